```python
import jax, jax.numpy as jnp
from jax import lax
import numpy as np

D_MODEL = 1024
BATCH = 16
SEQ = 2048
DEPTH = 2

N_EVEN = (DEPTH + 1) // 2
N_ODD = DEPTH // 2
D_FF = 2816
NORM_EPS = 1e-6
BLOCK = 128

A_HEADS = 8
A_HEAD_DIM = 64
A_BRANCHES = ((128, 1), (512, 4), (2048, 16))
B_HEADS = 8
B_NOPE_DIM = 64
B_ROPE_DIM = 32
B_V_DIM = 64
B_Q_RANK = 256
B_KV_RANK = 128
ROPE_THETA = 10000.0
A_QKV = 3 * A_HEADS * A_HEAD_DIM
AB_IN = A_QKV + B_Q_RANK + B_KV_RANK + B_ROPE_DIM
AB_OUT = A_HEADS * A_HEAD_DIM + B_HEADS * B_V_DIM
C_HEADS = 16
C_HEAD_DIM = 64
C_IN = 3 * C_HEADS * C_HEAD_DIM
C_OUT = C_HEADS * C_HEAD_DIM

kernel_name = 'hybrid_dilated_mla_stickbreak_macaron'


def rmsnorm(x, g):
    xf = x.astype(jnp.float32)
    y = xf * lax.rsqrt(jnp.mean(xf * xf, axis=-1, keepdims=True) + NORM_EPS)
    return (y * g.astype(jnp.float32)).astype(x.dtype)


def swiglu(x, w_gate, w_up, w_down):
    return (jax.nn.silu(x @ w_gate) * (x @ w_up)) @ w_down


def rope(x):
    s, dr = x.shape[1], x.shape[-1]
    inv = ROPE_THETA ** (-jnp.arange(0, dr, 2, dtype=jnp.float32) / dr)
    ang = jnp.arange(s, dtype=jnp.float32)[:, None] * inv[None, :]
    cos = jnp.cos(ang)[None, :, None, :]
    sin = jnp.sin(ang)[None, :, None, :]
    xf = x.astype(jnp.float32)
    x1, x2 = xf[..., : dr // 2], xf[..., dr // 2:]
    return jnp.concatenate([x1 * cos - x2 * sin, x1 * sin + x2 * cos], axis=-1).astype(x.dtype)


def dilated_branch(q, k, v, window, dilation, slopes):
    b, s, h, dh = q.shape
    n_back = window // dilation
    L = s // dilation
    nb = -(-L // BLOCK)
    Lp = nb * BLOCK

    def to_sub(t):
        t = t.reshape(b, L, dilation, h, dh)
        t = jnp.pad(t, ((0, 0), (0, Lp - L), (0, 0), (0, 0), (0, 0)))
        return t.reshape(b, nb, BLOCK, dilation, h, dh)

    def band(t):
        prev = jnp.pad(t, ((0, 0), (1, 0), (0, 0), (0, 0), (0, 0), (0, 0)))[:, :-1]
        return jnp.concatenate([prev, t], axis=2)

    qs = to_sub(q)
    kb = band(to_sub(k))
    vb = band(to_sub(v)).astype(jnp.float32)
    sc = jnp.einsum('bnqrhd,bnkrhd->bnrhqk', qs, kb, preferred_element_type=jnp.float32) * dh ** -0.5
    qi = jnp.arange(BLOCK)[:, None]
    kk = jnp.arange(2 * BLOCK)[None, :]
    rel = qi + BLOCK - kk
    j_abs = jnp.arange(nb)[:, None, None] * BLOCK - BLOCK + kk[None]
    valid = (rel >= 0)[None] & (rel <= n_back)[None] & (j_abs >= 0)
    bias = -slopes[:, None, None] * (rel * dilation).astype(jnp.float32)[None]
    sc = jnp.where(valid[None, :, None, None], sc + bias[None, None, None], -jnp.inf)
    m = jnp.max(sc, axis=-1)
    p = jnp.exp(sc - m[..., None])
    l = jnp.sum(p, axis=-1)
    o = jnp.einsum('bnrhqk,bnkrhd->bnqrhd', p, vb)
    m = jnp.moveaxis(m, -1, 2)
    l = jnp.moveaxis(l, -1, 2)
    o = o / l[..., None]
    o = o.reshape(b, Lp, dilation, h, dh)[:, :L].reshape(b, s, h, dh)
    m = m.reshape(b, Lp, dilation, h)[:, :L].reshape(b, s, h)
    l = l.reshape(b, Lp, dilation, h)[:, :L].reshape(b, s, h)
    return o, m, l


def dilated_attention(q, k, v):
    slopes = 2.0 ** (-8.0 * jnp.arange(1, A_HEADS + 1, dtype=jnp.float32) / A_HEADS)
    outs = [dilated_branch(q, k, v, w, d, slopes) for (w, d) in A_BRANCHES]
    o_all = jnp.stack([o for o, _, _ in outs])
    m_all = jnp.stack([m for _, m, _ in outs])
    l_all = jnp.stack([l for _, _, l in outs])
    den = l_all * jnp.exp(m_all - jnp.max(m_all, axis=0, keepdims=True))
    alpha = den / jnp.sum(den, axis=0, keepdims=True)
    return jnp.sum(alpha[..., None] * o_all, axis=0)


def to_blocks(t):
    b, s = t.shape[0], t.shape[1]
    return jnp.moveaxis(t.reshape((b, s // BLOCK, BLOCK) + t.shape[2:]), 1, 0)


def from_blocks(t):
    t = jnp.moveaxis(t, 0, 1)
    return t.reshape((t.shape[0], t.shape[1] * t.shape[2]) + t.shape[3:])


def mla_attention(qn, qr, kn, kr, v):
    s = qn.shape[1]
    scale = (B_NOPE_DIM + B_ROPE_DIM) ** -0.5
    kpos = jnp.arange(s)
    vf = v.astype(jnp.float32)

    def one(args):
        qn_b, qr_b, idx = args
        sc = (jnp.einsum('bqhd,bkhd->bhqk', qn_b, kn, preferred_element_type=jnp.float32)
              + jnp.einsum('bqhd,bkd->bhqk', qr_b, kr, preferred_element_type=jnp.float32)) * scale
        qpos = idx * BLOCK + jnp.arange(BLOCK)
        sc = jnp.where(kpos[None, :] <= qpos[:, None], sc, -jnp.inf)
        p = jax.nn.softmax(sc, axis=-1)
        return jnp.einsum('bhqk,bkhd->bqhd', p, vf)

    out = lax.map(one, (to_blocks(qn), to_blocks(qr), jnp.arange(s // BLOCK)))
    return from_blocks(out)


def stick_breaking_attention(q, k, v):
    s, dh = q.shape[1], q.shape[-1]
    kpos = jnp.arange(s)
    vf = v.astype(jnp.float32)

    def one(args):
        q_b, idx = args
        z = jnp.einsum('bqhd,bkhd->bhqk', q_b, k, preferred_element_type=jnp.float32) * dh ** -0.5
        qpos = idx * BLOCK + jnp.arange(BLOCK)
        strict = kpos[None, :] < qpos[:, None]
        log1m = jnp.where(strict, -jax.nn.softplus(z), 0.0)
        later = lax.cumsum(log1m, axis=3, reverse=True) - log1m
        a = jnp.where(strict, jnp.exp(jax.nn.log_sigmoid(z) + later), 0.0)
        return jnp.einsum('bhqk,bkhd->bqhd', a, vf)

    out = lax.map(one, (to_blocks(q), jnp.arange(s // BLOCK)))
    return from_blocks(out)


def mixer_ab(h, w_in, q_norm_g, w_uq, kv_norm_g, w_ukv, w_out):
    b, s, _ = h.shape
    proj = h @ w_in
    a_qkv = proj[..., :A_QKV].reshape(b, s, 3, A_HEADS, A_HEAD_DIM)
    c_q = proj[..., A_QKV:A_QKV + B_Q_RANK]
    c_kv = proj[..., A_QKV + B_Q_RANK:A_QKV + B_Q_RANK + B_KV_RANK]
    k_rope = proj[..., A_QKV + B_Q_RANK + B_KV_RANK:]
    o_a = dilated_attention(a_qkv[:, :, 0], a_qkv[:, :, 1], a_qkv[:, :, 2])
    q = (rmsnorm(c_q, q_norm_g) @ w_uq).reshape(b, s, B_HEADS, B_NOPE_DIM + B_ROPE_DIM)
    kv = (rmsnorm(c_kv, kv_norm_g) @ w_ukv).reshape(b, s, B_HEADS, B_NOPE_DIM + B_V_DIM)
    qn, qr = q[..., :B_NOPE_DIM], rope(q[..., B_NOPE_DIM:])
    kn, vb = kv[..., :B_NOPE_DIM], kv[..., B_NOPE_DIM:]
    kr = rope(k_rope[:, :, None, :])[:, :, 0]
    o_b = mla_attention(qn, qr, kn, kr, vb)
    o = jnp.concatenate([o_a.reshape(b, s, A_HEADS * A_HEAD_DIM),
                         o_b.reshape(b, s, B_HEADS * B_V_DIM)], axis=-1).astype(h.dtype)
    return o @ w_out


def mixer_c(h, w_in, w_out):
    b, s, _ = h.shape
    qkv = (h @ w_in).reshape(b, s, 3, C_HEADS, C_HEAD_DIM)
    o = stick_breaking_attention(qkv[:, :, 0], qkv[:, :, 1], qkv[:, :, 2])
    return o.reshape(b, s, C_OUT).astype(h.dtype) @ w_out


def setup_inputs(seed: int = 0) -> dict:
    key = jax.random.key(seed)
    ks = jax.random.split(key, 16)
    f32 = jnp.float32

    def nrm(k, shape, fan_in):
        return jax.random.normal(k, shape, f32) * fan_in ** -0.5

    def gain(k, shape):
        return 1.0 + 0.02 * jax.random.normal(k, shape, f32)

    return {
        'x': jax.random.normal(ks[0], (BATCH, SEQ, D_MODEL), f32),
        'ffn_norm_g': gain(ks[1], (DEPTH, 2, D_MODEL)),
        'mix_norm_g': gain(ks[2], (DEPTH, D_MODEL)),
        'ffn_w_gate': nrm(ks[3], (DEPTH, 2, D_MODEL, D_FF), D_MODEL),
        'ffn_w_up': nrm(ks[4], (DEPTH, 2, D_MODEL, D_FF), D_MODEL),
        'ffn_w_down': nrm(ks[5], (DEPTH, 2, D_FF, D_MODEL), D_FF),
        'ab_w_in': nrm(ks[6], (N_EVEN, D_MODEL, AB_IN), D_MODEL),
        'mla_q_norm_g': gain(ks[7], (N_EVEN, B_Q_RANK)),
        'mla_w_uq': nrm(ks[8], (N_EVEN, B_Q_RANK, B_HEADS * (B_NOPE_DIM + B_ROPE_DIM)), B_Q_RANK),
        'mla_kv_norm_g': gain(ks[9], (N_EVEN, B_KV_RANK)),
        'mla_w_ukv': nrm(ks[10], (N_EVEN, B_KV_RANK, B_HEADS * (B_NOPE_DIM + B_V_DIM)), B_KV_RANK),
        'ab_w_out': nrm(ks[11], (N_EVEN, AB_OUT, D_MODEL), AB_OUT),
        'sb_w_in': nrm(ks[12], (N_ODD, D_MODEL, C_IN), D_MODEL),
        'sb_w_out': nrm(ks[13], (N_ODD, C_OUT, D_MODEL), C_OUT),
        'final_norm_g': gain(ks[14], (D_MODEL,)),
    }


def reference(x, ffn_norm_g, mix_norm_g, ffn_w_gate, ffn_w_up, ffn_w_down, ab_w_in, mla_q_norm_g,
              mla_w_uq, mla_kv_norm_g, mla_w_ukv, ab_w_out, sb_w_in, sb_w_out, final_norm_g):
    for i in range(DEPTH):
        x = x + 0.5 * swiglu(rmsnorm(x, ffn_norm_g[i, 0]), ffn_w_gate[i, 0], ffn_w_up[i, 0], ffn_w_down[i, 0])
        h = rmsnorm(x, mix_norm_g[i])
        if i % 2 == 0:
            e = i // 2
            x = x + mixer_ab(h, ab_w_in[e], mla_q_norm_g[e], mla_w_uq[e], mla_kv_norm_g[e], mla_w_ukv[e], ab_w_out[e])
        else:
            o = i // 2
            x = x + mixer_c(h, sb_w_in[o], sb_w_out[o])
        x = x + 0.5 * swiglu(rmsnorm(x, ffn_norm_g[i, 1]), ffn_w_gate[i, 1], ffn_w_up[i, 1], ffn_w_down[i, 1])
    return rmsnorm(x, final_norm_g)
```

```python
import functools

import numpy as np
import jax
import jax.numpy as jnp
from jax import lax
from jax.experimental import pallas as pl
from jax.experimental.pallas import tpu as pltpu

F32 = jnp.float32
BF16 = jnp.bfloat16

NORM_EPS = 1e-6
ROPE_THETA = 10000.0
LANES = 128
HEAD_DIM = 64
BLOCK = 128
A_HEADS = 8
A_BRANCHES = ((128, 1), (512, 4), (2048, 16))
B_HEADS = 8
B_NOPE = 64
B_ROPE = 32
B_Q_RANK = 256
B_KV_RANK = 128
C_HEADS = 16
NEG_BIG = -1e30
VMEM_LIMIT = 58 * 1024 * 1024

_NT = (((1,), (1,)), ((), ()))


def _rms(x, g):
    return x * lax.rsqrt(jnp.mean(x * x, axis=-1, keepdims=True) + NORM_EPS) * g


def _const_spec(shape):
    nd = len(shape)
    return pl.BlockSpec(shape, lambda *_: (0,) * nd, pipeline_mode=pl.Buffered(1))


def _params(sem):
    return pltpu.CompilerParams(dimension_semantics=sem, vmem_limit_bytes=VMEM_LIMIT)


def _ffn_kernel(*refs, n_pre, final_norm, tf):
    x_ref = refs[0]
    pre = refs[1:1 + 2 * n_pre]
    g_ref, wg_ref, wu_ref, wd_ref = refs[1 + 2 * n_pre:5 + 2 * n_pre]
    rest = refs[5 + 2 * n_pre:]
    fg_ref = rest[0] if final_norm else None
    o_ref = rest[-1]

    x = x_ref[...]
    for p in range(n_pre):
        x = x + jnp.dot(pre[2 * p][...], pre[2 * p + 1][...], preferred_element_type=F32)
    h = _rms(x, g_ref[...]).astype(BF16)
    d_ff = wg_ref.shape[1]
    acc = None
    for c in range(d_ff // tf):
        sl = slice(c * tf, (c + 1) * tf)
        gate = jnp.dot(h, wg_ref[:, sl], preferred_element_type=F32)
        up = jnp.dot(h, wu_ref[:, sl], preferred_element_type=F32)
        a = (gate * jax.nn.sigmoid(gate) * up).astype(BF16)
        d = jnp.dot(a, wd_ref[sl, :], preferred_element_type=F32)
        acc = d if acc is None else acc + d
    y = x + 0.5 * acc
    if final_norm:
        y = _rms(y, fg_ref[...])
    o_ref[...] = y


def _ffn(x, pre, g, wg, wu, wd, final_g=None, *, tm=512, tf=256):
    n, d = x.shape
    row = lambda i: (i, 0)
    args = [x]
    specs = [pl.BlockSpec((tm, d), row)]
    for o, w in pre:
        args += [o, w]
        specs += [pl.BlockSpec((tm, o.shape[1]), row), _const_spec(w.shape)]
    args += [g.reshape(1, d), wg, wu, wd]
    specs += [_const_spec((1, d)), _const_spec(wg.shape), _const_spec(wu.shape), _const_spec(wd.shape)]
    if final_g is not None:
        args.append(final_g.reshape(1, d))
        specs.append(_const_spec((1, d)))
    return pl.pallas_call(
        functools.partial(_ffn_kernel, n_pre=len(pre), final_norm=final_g is not None, tf=tf),
        grid=(n // tm,),
        in_specs=specs,
        out_specs=pl.BlockSpec((tm, d), row),
        out_shape=jax.ShapeDtypeStruct((n, d), F32),
        compiler_params=_params(("parallel",)),
        name="ffn",
    )(*args)


def _ab_in_kernel(x_ref, g_ref, win_ref, qg_ref, wuq_ref, kvg_ref, wukv_ref,
                  cq_ref, sq_ref, ck_ref, sk_ref,
                  a_ref, qn_ref, qr_ref, kn_ref, kr_ref, v_ref, *, scale):
    h = _rms(x_ref[...], g_ref[...]).astype(BF16)
    p = jnp.dot(h, win_ref[...], preferred_element_type=F32)
    na = a_ref.shape[1]
    a_ref[...] = p[:, :na]
    c_q = p[:, na:na + B_Q_RANK]
    c_kv = p[:, na + B_Q_RANK:na + B_Q_RANK + B_KV_RANK]
    k0 = na + B_Q_RANK + B_KV_RANK
    kr_ref[...] = (p[:, k0:k0 + LANES] * ck_ref[...] + p[:, k0 + LANES:k0 + 2 * LANES] * sk_ref[...]).astype(BF16)
    q = jnp.dot(_rms(c_q, qg_ref[...]).astype(BF16), wuq_ref[...], preferred_element_type=F32)
    nn = qn_ref.shape[1]
    nr = qr_ref.shape[1]
    qn_ref[...] = (q[:, :nn] * scale).astype(BF16)
    qr_ref[...] = ((q[:, nn:nn + nr] * cq_ref[...] + q[:, nn + nr:nn + 2 * nr] * sq_ref[...]) * scale).astype(BF16)
    kv = jnp.dot(_rms(c_kv, kvg_ref[...]).astype(BF16), wukv_ref[...], preferred_element_type=F32)
    kn_ref[...] = kv[:, :nn].astype(BF16)
    v_ref[...] = kv[:, nn:].astype(BF16)


def _rot_half_cols(w, width):
    k, n = w.shape
    w3 = w.reshape(k, n // width, 2, width // 2)
    return jnp.stack([-w3[:, :, 1], w3[:, :, 0]], axis=2).reshape(k, n)


def _ab_in_proj(x, seq, g, w_in, q_g, w_uq, kv_g, w_ukv, *, tm=512):
    n, d = x.shape
    a_cols = 3 * A_HEADS * HEAD_DIM
    a_scale = HEAD_DIM ** -0.5
    w_a = jnp.concatenate([w_in[:, :A_HEADS * HEAD_DIM] * a_scale, w_in[:, A_HEADS * HEAD_DIM:a_cols]], axis=1)
    w_lat = w_in[:, a_cols:a_cols + B_Q_RANK + B_KV_RANK]
    w_kr = w_in[:, a_cols + B_Q_RANK + B_KV_RANK:]
    reps = LANES // B_ROPE
    w_full = jnp.concatenate(
        [w_a, w_lat, jnp.tile(w_kr, (1, reps)), jnp.tile(_rot_half_cols(w_kr, B_ROPE), (1, reps))], axis=1).astype(BF16)
    uq = w_uq.reshape(B_Q_RANK, B_HEADS, B_NOPE + B_ROPE)
    uq_n = uq[:, :, :B_NOPE].reshape(B_Q_RANK, B_HEADS * B_NOPE)
    uq_r = uq[:, :, B_NOPE:].reshape(B_Q_RANK, B_HEADS * B_ROPE)
    w_uq_full = jnp.concatenate([uq_n, uq_r, _rot_half_cols(uq_r, B_ROPE)], axis=1).astype(BF16)
    ukv = w_ukv.reshape(B_KV_RANK, B_HEADS, 2, B_NOPE)
    w_ukv_full = jnp.concatenate([ukv[:, :, 0].reshape(B_KV_RANK, -1), ukv[:, :, 1].reshape(B_KV_RANK, -1)], axis=1).astype(BF16)
    inv = ROPE_THETA ** (-jnp.arange(0, B_ROPE, 2, dtype=F32) / B_ROPE)
    ang = jnp.arange(seq, dtype=F32)[:, None] * inv[None, :]
    cos2 = jnp.concatenate([jnp.cos(ang)] * 2, axis=1)
    sin2 = jnp.concatenate([jnp.sin(ang)] * 2, axis=1)
    cq, sq = jnp.tile(cos2, (1, B_HEADS)), jnp.tile(sin2, (1, B_HEADS))
    ck, sk = jnp.tile(cos2, (1, reps)), jnp.tile(sin2, (1, reps))

    row = lambda i: (i, 0)
    per_seq = seq // tm
    pos = lambda i: (i % per_seq, 0)
    nq_n = B_HEADS * B_NOPE
    nq_r = B_HEADS * B_ROPE
    outs = [(a_cols, F32), (nq_n, BF16), (nq_r, BF16), (nq_n, BF16), (LANES, BF16), (nq_n, BF16)]
    return pl.pallas_call(
        functools.partial(_ab_in_kernel, scale=(B_NOPE + B_ROPE) ** -0.5),
        grid=(n // tm,),
        in_specs=[pl.BlockSpec((tm, d), row), _const_spec((1, d)), _const_spec(w_full.shape),
                  _const_spec((1, B_Q_RANK)), _const_spec(w_uq_full.shape),
                  _const_spec((1, B_KV_RANK)), _const_spec(w_ukv_full.shape),
                  pl.BlockSpec((tm, nq_r), pos), pl.BlockSpec((tm, nq_r), pos),
                  pl.BlockSpec((tm, LANES), pos), pl.BlockSpec((tm, LANES), pos)],
        out_specs=[pl.BlockSpec((tm, c), row) for c, _ in outs],
        out_shape=[jax.ShapeDtypeStruct((n, c), dt) for c, dt in outs],
        compiler_params=_params(("parallel",)),
        name="ab_in_proj",
    )(x, g.reshape(1, d), w_full, q_g.reshape(1, -1), w_uq_full, kv_g.reshape(1, -1), w_ukv_full, cq, sq, ck, sk)


def _a_block(q, kc, kp, vc, vp, bias_c, bias_p, pen):
    s_c = lax.dot_general(q, kc, _NT, preferred_element_type=F32) + bias_c
    m = jnp.max(s_c, axis=-1, keepdims=True)
    if kp is not None:
        s_p = lax.dot_general(q, kp, _NT, preferred_element_type=F32) + bias_p + pen
        m = jnp.maximum(m, jnp.max(s_p, axis=-1, keepdims=True))
    p_c = jnp.exp(s_c - m)
    l = jnp.sum(p_c, axis=-1, keepdims=True)
    acc = jnp.dot(p_c.astype(BF16), vc, preferred_element_type=F32)
    if kp is not None:
        p_p = jnp.exp(s_p - m)
        l = l + jnp.sum(p_p, axis=-1, keepdims=True)
        acc = acc + jnp.dot(p_p.astype(BF16), vp, preferred_element_type=F32)
    return m, l, acc


def _dilated_kernel(q_ref, k_ref, v_ref, bias_ref, o_ref, qd, kd, vd, md, ld, ad, ms, ls, accs):
    seq = q_ref.shape[0]
    lane = lax.broadcasted_iota(jnp.int32, (1, LANES), 1)
    lo = lane < HEAD_DIM

    for bi, (_, dil) in enumerate(A_BRANCHES):
        sub = seq // dil
        nb = -(-sub // BLOCK)
        for r in range(dil):
            rows = pl.ds(r, sub, stride=dil) if dil > 1 else slice(None)
            qd[r * sub:(r + 1) * sub, :] = q_ref[rows, :].astype(BF16)
            kd[r * sub:(r + 1) * sub, :] = k_ref[rows, :].astype(BF16)
            vd[r * sub:(r + 1) * sub, :] = v_ref[rows, :].astype(BF16)

        def block(idx, _, bi=bi, nb=nb):
            base = pl.multiple_of(idx * BLOCK, BLOCK)
            qb = qd[pl.ds(base, BLOCK), :]
            kc = kd[pl.ds(base, BLOCK), :]
            vc = vd[pl.ds(base, BLOCK), :]
            if nb > 1:
                first = (idx % nb) == 0
                prev = pl.multiple_of(jnp.maximum(base - BLOCK, 0), BLOCK)
                kp = kd[pl.ds(prev, BLOCK), :]
                vp = vd[pl.ds(prev, BLOCK), :]
                pen = jnp.where(first, NEG_BIG, 0.0).astype(F32)
            else:
                kp = vp = pen = None
            res = []
            for e in range(2):
                qe = jnp.where(lo if e == 0 else ~lo, qb, jnp.zeros_like(qb))
                res.append(_a_block(qe, kc, kp, vc, vp, bias_ref[bi, e, :, BLOCK:],
                                    bias_ref[bi, e, :, :BLOCK], pen))
            md[pl.ds(base, BLOCK), :] = jnp.where(lo, res[0][0], res[1][0])
            ld[pl.ds(base, BLOCK), :] = jnp.where(lo, res[0][1], res[1][1])
            ad[pl.ds(base, BLOCK), :] = jnp.where(lo, res[0][2], res[1][2])
            return 0

        lax.fori_loop(0, seq // BLOCK, block, 0)

        for r in range(dil):
            rows = pl.ds(r, sub, stride=dil) if dil > 1 else slice(None)
            ms[bi, rows, :] = md[r * sub:(r + 1) * sub, :]
            ls[bi, rows, :] = ld[r * sub:(r + 1) * sub, :]
            accs[bi, rows, :] = ad[r * sub:(r + 1) * sub, :]

    nbr = len(A_BRANCHES)

    def merge(idx, _):
        rows = pl.ds(pl.multiple_of(idx * BLOCK, BLOCK), BLOCK)
        m_all = [ms[i, rows, :] for i in range(nbr)]
        m_max = functools.reduce(jnp.maximum, m_all)
        w = [jnp.exp(m - m_max) for m in m_all]
        num = sum(w[i] * accs[i, rows, :] for i in range(nbr))
        den = sum(w[i] * ls[i, rows, :] for i in range(nbr))
        o_ref[rows, :] = (num / den).astype(BF16)
        return 0

    lax.fori_loop(0, seq // BLOCK, merge, 0)


def _alibi_bias():
    slopes = 2.0 ** (-8.0 * np.arange(1, A_HEADS + 1, dtype=np.float64) / A_HEADS)
    qi = np.arange(BLOCK)[:, None]
    kk = np.arange(2 * BLOCK)[None, :]
    rel = qi + BLOCK - kk
    out = np.empty((len(A_BRANCHES), A_HEADS, BLOCK, 2 * BLOCK), np.float32)
    for bi, (window, dil) in enumerate(A_BRANCHES):
        valid = (rel >= 0) & (rel <= window // dil)
        bias = -slopes[:, None, None] * (rel * dil)[None]
        out[bi] = np.where(valid[None], bias, NEG_BIG)
    return jnp.asarray(out)


def _dilated_attn(a_qkv, batch, seq):
    a3 = a_qkv.reshape(batch, seq, a_qkv.shape[1])
    pairs = A_HEADS * HEAD_DIM // LANES
    bias = _alibi_bias()
    blk = lambda off: pl.BlockSpec((None, seq, LANES), lambda j, b: (b, 0, off + j))
    nbr = len(A_BRANCHES)
    out = pl.pallas_call(
        _dilated_kernel,
        grid=(pairs, batch),
        in_specs=[blk(0), blk(pairs), blk(2 * pairs),
                  pl.BlockSpec((nbr, 2, BLOCK, 2 * BLOCK), lambda j, b: (0, j, 0, 0))],
        out_specs=pl.BlockSpec((None, seq, LANES), lambda j, b: (b, 0, j)),
        out_shape=jax.ShapeDtypeStruct((batch, seq, pairs * LANES), BF16),
        scratch_shapes=[pltpu.VMEM((seq, LANES), BF16)] * 3 + [pltpu.VMEM((seq, LANES), F32)] * 3
                       + [pltpu.VMEM((nbr, seq, LANES), F32)] * 3,
        compiler_params=_params(("parallel", "parallel")),
        name="dilated_attn",
    )(a3, a3, a3, bias)
    return out.reshape(batch * seq, pairs * LANES)


def _mla_kernel(qn_ref, qr_ref, kn_ref, kr_ref, v_ref, o_ref, *, tq):
    j = pl.program_id(1)
    i = pl.program_id(2)
    lane = lax.broadcasted_iota(jnp.int32, (1, LANES), 1)
    lo = lane < HEAD_DIM
    qn = qn_ref[...]
    qr = qr_ref[...]
    row = lax.broadcasted_iota(jnp.int32, (tq, tq), 0)
    col = lax.broadcasted_iota(jnp.int32, (tq, tq), 1)
    causal = col <= row

    outs = []
    for e in range(2):
        slot = (2 * j + e) % (LANES // B_ROPE)
        in_slot = (lane >= slot * B_ROPE) & (lane < (slot + 1) * B_ROPE)
        q = jnp.concatenate([jnp.where(lo if e == 0 else ~lo, qn, jnp.zeros_like(qn)),
                             jnp.where(in_slot, qr, jnp.zeros_like(qr))], axis=1)

        def step(kb, carry, masked, q=q):
            m, l, acc = carry
            rows = pl.ds(pl.multiple_of(kb * tq, tq), tq)
            k = jnp.concatenate([kn_ref[rows, :], kr_ref[rows, :]], axis=1)
            s = lax.dot_general(q, k, _NT, preferred_element_type=F32)
            if masked:
                s = jnp.where(causal, s, NEG_BIG)
            m_new = jnp.maximum(m, jnp.max(s, axis=-1, keepdims=True))
            alpha = jnp.exp(m - m_new)
            p = jnp.exp(s - m_new)
            l = alpha * l + jnp.sum(p, axis=-1, keepdims=True)
            acc = alpha * acc + jnp.dot(p.astype(BF16), v_ref[rows, :], preferred_element_type=F32)
            return m_new, l, acc

        init = (jnp.full((tq, 1), NEG_BIG, F32), jnp.zeros((tq, 1), F32), jnp.zeros((tq, LANES), F32))
        carry = lax.fori_loop(0, i, functools.partial(step, masked=False), init)
        _, l, acc = step(i, carry, True)
        outs.append(acc / l)
    o_ref[...] = jnp.where(lo, outs[0], outs[1]).astype(BF16)


def _mla_attn(qn, qr, kn, kr, v, batch, seq, *, tq=256):
    pairs = B_HEADS * B_NOPE // LANES
    per_quad = LANES // B_ROPE // 2
    r3 = lambda t: t.reshape(batch, seq, t.shape[1])
    out = pl.pallas_call(
        functools.partial(_mla_kernel, tq=tq),
        grid=(batch, pairs, seq // tq),
        in_specs=[pl.BlockSpec((None, tq, LANES), lambda b, j, i: (b, i, j)),
                  pl.BlockSpec((None, tq, LANES), lambda b, j, i: (b, i, j // per_quad)),
                  pl.BlockSpec((None, seq, LANES), lambda b, j, i: (b, 0, j)),
                  pl.BlockSpec((None, seq, LANES), lambda b, j, i: (b, 0, 0)),
                  pl.BlockSpec((None, seq, LANES), lambda b, j, i: (b, 0, j))],
        out_specs=pl.BlockSpec((None, tq, LANES), lambda b, j, i: (b, i, j)),
        out_shape=jax.ShapeDtypeStruct((batch, seq, pairs * LANES), BF16),
        compiler_params=_params(("parallel", "parallel", "arbitrary")),
        name="mla_attn",
    )(r3(qn), r3(qr), r3(kn), r3(kr), r3(v))
    return out.reshape(batch * seq, pairs * LANES)


def _norm_proj_kernel(x_ref, g_ref, w_ref, o_ref):
    h = _rms(x_ref[...], g_ref[...]).astype(BF16)
    o_ref[...] = jnp.dot(h, w_ref[...], preferred_element_type=F32).astype(o_ref.dtype)


def _sb_in_proj(x, g, w_in, *, tm=512):
    n, d = x.shape
    nq = C_HEADS * HEAD_DIM
    w = jnp.concatenate([w_in[:, :nq] * HEAD_DIM ** -0.5, w_in[:, nq:]], axis=1).astype(BF16)
    return pl.pallas_call(
        _norm_proj_kernel,
        grid=(n // tm,),
        in_specs=[pl.BlockSpec((tm, d), lambda i: (i, 0)), _const_spec((1, d)), _const_spec(w.shape)],
        out_specs=pl.BlockSpec((tm, w.shape[1]), lambda i: (i, 0)),
        out_shape=jax.ShapeDtypeStruct((n, w.shape[1]), BF16),
        compiler_params=_params(("parallel",)),
        name="sb_in_proj",
    )(x, g.reshape(1, d), w)


def _sb_kernel(q_ref, k_ref, v_ref, tri_ref, o_ref, *, tq):
    i = pl.program_id(2)
    lane = lax.broadcasted_iota(jnp.int32, (1, LANES), 1)
    lo = lane < HEAD_DIM
    qb = q_ref[...]
    tri = tri_ref[...]
    row = lax.broadcasted_iota(jnp.int32, (tq, tq), 0)
    col = lax.broadcasted_iota(jnp.int32, (tq, tq), 1)
    strict = col < row

    outs = []
    for e in range(2):
        q = jnp.where(lo if e == 0 else ~lo, qb, jnp.zeros_like(qb))

        def step(kb, carry, masked, q=q):
            later_blocks, acc = carry
            rows = pl.ds(pl.multiple_of(kb * tq, tq), tq)
            z = lax.dot_general(q, k_ref[rows, :], _NT, preferred_element_type=F32)
            sp = jnp.maximum(z, 0.0) + jnp.log(1.0 + jnp.exp(-jnp.abs(z)))
            log1m = -sp
            if masked:
                log1m = jnp.where(strict, log1m, 0.0)
            hi = log1m.astype(BF16)
            lo_part = (log1m - hi.astype(F32)).astype(BF16)
            incl = (jnp.dot(hi, tri, preferred_element_type=F32)
                    + jnp.dot(lo_part, tri, preferred_element_type=F32))
            a = jnp.exp(z - sp + (later_blocks + incl - log1m))
            if masked:
                a = jnp.where(strict, a, 0.0)
            acc = acc + jnp.dot(a.astype(BF16), v_ref[rows, :], preferred_element_type=F32)
            return later_blocks + incl[:, 0:1], acc

        carry = step(i, (jnp.zeros((tq, 1), F32), jnp.zeros((tq, LANES), F32)), True)
        _, acc = lax.fori_loop(0, i, lambda t, c: step(i - 1 - t, c, False), carry)
        outs.append(acc)
    o_ref[...] = jnp.where(lo, outs[0], outs[1]).astype(BF16)


def _sb_attn(qkv, batch, seq, *, tq=128):
    pairs = C_HEADS * HEAD_DIM // LANES
    q3 = qkv.reshape(batch, seq, qkv.shape[1])
    tri = jnp.asarray(np.tril(np.ones((tq, tq), np.float32)), BF16)
    out = pl.pallas_call(
        functools.partial(_sb_kernel, tq=tq),
        grid=(batch, pairs, seq // tq),
        in_specs=[pl.BlockSpec((None, tq, LANES), lambda b, j, i: (b, i, j)),
                  pl.BlockSpec((None, seq, LANES), lambda b, j, i: (b, 0, pairs + j)),
                  pl.BlockSpec((None, seq, LANES), lambda b, j, i: (b, 0, 2 * pairs + j)),
                  pl.BlockSpec((tq, tq), lambda b, j, i: (0, 0))],
        out_specs=pl.BlockSpec((None, tq, LANES), lambda b, j, i: (b, i, j)),
        out_shape=jax.ShapeDtypeStruct((batch, seq, pairs * LANES), BF16),
        compiler_params=_params(("parallel", "parallel", "arbitrary")),
        name="sb_attn",
    )(q3, q3, q3, tri)
    return out.reshape(batch * seq, pairs * LANES)


def kernel(x, ffn_norm_g, mix_norm_g, ffn_w_gate, ffn_w_up, ffn_w_down, ab_w_in, mla_q_norm_g,
           mla_w_uq, mla_kv_norm_g, mla_w_ukv, ab_w_out, sb_w_in, sb_w_out, final_norm_g):
    batch, seq, d = x.shape
    depth = ffn_norm_g.shape[0]
    h = x.reshape(batch * seq, d)
    bf = lambda w: w.astype(BF16)

    def ffn(h, pre, i, s, final_g=None):
        return _ffn(h, pre, ffn_norm_g[i, s], bf(ffn_w_gate[i, s]), bf(ffn_w_up[i, s]), bf(ffn_w_down[i, s]), final_g)

    for i in range(depth):
        h = ffn(h, [], i, 0)
        if i % 2 == 0:
            e = i // 2
            a_qkv, qn, qr, kn, kr, v = _ab_in_proj(h, seq, mix_norm_g[i], ab_w_in[e], mla_q_norm_g[e],
                                                   mla_w_uq[e], mla_kv_norm_g[e], mla_w_ukv[e])
            o_a = _dilated_attn(a_qkv, batch, seq)
            o_b = _mla_attn(qn, qr, kn, kr, v, batch, seq)
            w_out = bf(ab_w_out[e])
            na = o_a.shape[1]
            pre = [(o_a, w_out[:na]), (o_b, w_out[na:])]
        else:
            o = i // 2
            qkv = _sb_in_proj(h, mix_norm_g[i], sb_w_in[o])
            pre = [(_sb_attn(qkv, batch, seq), bf(sb_w_out[o]))]
        h = ffn(h, pre, i, 1, final_norm_g if i == depth - 1 else None)
    return h.reshape(batch, seq, d)
```

```python
import functools

import numpy as np
import jax
import jax.numpy as jnp
from jax import lax
from jax.experimental import pallas as pl
from jax.experimental.pallas import tpu as pltpu

F32 = jnp.float32
BF16 = jnp.bfloat16

NORM_EPS = 1e-6
ROPE_THETA = 10000.0
LANES = 128
HEAD_DIM = 64
BLOCK = 128
A_HEADS = 8
A_BRANCHES = ((128, 1), (512, 4), (2048, 16))
B_HEADS = 8
B_NOPE = 64
B_ROPE = 32
B_Q_RANK = 256
B_KV_RANK = 128
C_HEADS = 16
NEG_BIG = -1e30
VMEM_LIMIT = 58 * 1024 * 1024

_NT = (((1,), (1,)), ((), ()))


def _rms(x, g):
    return x * lax.rsqrt(jnp.mean(x * x, axis=-1, keepdims=True) + NORM_EPS) * g


def _const_spec(shape):
    nd = len(shape)
    return pl.BlockSpec(shape, lambda *_: (0,) * nd, pipeline_mode=pl.Buffered(1))


def _params(sem):
    return pltpu.CompilerParams(dimension_semantics=sem, vmem_limit_bytes=VMEM_LIMIT)


def _ffn_kernel(*refs, n_pre, final_norm, tf):
    x_ref = refs[0]
    pre = refs[1:1 + 2 * n_pre]
    g_ref, wg_ref, wu_ref, wd_ref = refs[1 + 2 * n_pre:5 + 2 * n_pre]
    rest = refs[5 + 2 * n_pre:]
    fg_ref = rest[0] if final_norm else None
    o_ref = rest[-1]

    x = x_ref[...]
    for p in range(n_pre):
        x = x + jnp.dot(pre[2 * p][...], pre[2 * p + 1][...], preferred_element_type=F32)
    h = _rms(x, g_ref[...]).astype(BF16)
    d_ff = wg_ref.shape[1]
    acc = None
    for c in range(d_ff // tf):
        sl = slice(c * tf, (c + 1) * tf)
        gate = jnp.dot(h, wg_ref[:, sl], preferred_element_type=F32)
        up = jnp.dot(h, wu_ref[:, sl], preferred_element_type=F32)
        a = (gate * jax.nn.sigmoid(gate) * up).astype(BF16)
        d = jnp.dot(a, wd_ref[sl, :], preferred_element_type=F32)
        acc = d if acc is None else acc + d
    y = x + 0.5 * acc
    if final_norm:
        y = _rms(y, fg_ref[...])
    o_ref[...] = y


def _ffn(x, pre, g, wg, wu, wd, final_g=None, *, tm=512, tf=256):
    n, d = x.shape
    row = lambda i: (i, 0)
    args = [x]
    specs = [pl.BlockSpec((tm, d), row)]
    for o, w in pre:
        args += [o, w]
        specs += [pl.BlockSpec((tm, o.shape[1]), row), _const_spec(w.shape)]
    args += [g.reshape(1, d), wg, wu, wd]
    specs += [_const_spec((1, d)), _const_spec(wg.shape), _const_spec(wu.shape), _const_spec(wd.shape)]
    if final_g is not None:
        args.append(final_g.reshape(1, d))
        specs.append(_const_spec((1, d)))
    return pl.pallas_call(
        functools.partial(_ffn_kernel, n_pre=len(pre), final_norm=final_g is not None, tf=tf),
        grid=(n // tm,),
        in_specs=specs,
        out_specs=pl.BlockSpec((tm, d), row),
        out_shape=jax.ShapeDtypeStruct((n, d), F32),
        compiler_params=_params(("parallel",)),
        name="ffn",
    )(*args)


def _ab_in_kernel(x_ref, g_ref, win_ref, qg_ref, wuq_ref, kvg_ref, wukv_ref,
                  cq_ref, sq_ref, ck_ref, sk_ref,
                  a_ref, qn_ref, qr_ref, kn_ref, kr_ref, v_ref, *, scale):
    h = _rms(x_ref[...], g_ref[...]).astype(BF16)
    p = jnp.dot(h, win_ref[...], preferred_element_type=F32)
    na = a_ref.shape[1]
    a_ref[...] = p[:, :na]
    c_q = p[:, na:na + B_Q_RANK]
    c_kv = p[:, na + B_Q_RANK:na + B_Q_RANK + B_KV_RANK]
    k0 = na + B_Q_RANK + B_KV_RANK
    kr_ref[...] = (p[:, k0:k0 + LANES] * ck_ref[...] + p[:, k0 + LANES:k0 + 2 * LANES] * sk_ref[...]).astype(BF16)
    q = jnp.dot(_rms(c_q, qg_ref[...]).astype(BF16), wuq_ref[...], preferred_element_type=F32)
    nn = qn_ref.shape[1]
    nr = qr_ref.shape[1]
    qn_ref[...] = (q[:, :nn] * scale).astype(BF16)
    qr_ref[...] = ((q[:, nn:nn + nr] * cq_ref[...] + q[:, nn + nr:nn + 2 * nr] * sq_ref[...]) * scale).astype(BF16)
    kv = jnp.dot(_rms(c_kv, kvg_ref[...]).astype(BF16), wukv_ref[...], preferred_element_type=F32)
    kn_ref[...] = kv[:, :nn].astype(BF16)
    v_ref[...] = kv[:, nn:].astype(BF16)


def _rot_half_cols(w, width):
    k, n = w.shape
    w3 = w.reshape(k, n // width, 2, width // 2)
    return jnp.stack([-w3[:, :, 1], w3[:, :, 0]], axis=2).reshape(k, n)


def _ab_in_proj(x, seq, g, w_in, q_g, w_uq, kv_g, w_ukv, *, tm=512):
    n, d = x.shape
    a_cols = 3 * A_HEADS * HEAD_DIM
    a_scale = HEAD_DIM ** -0.5
    w_a = jnp.concatenate([w_in[:, :A_HEADS * HEAD_DIM] * a_scale, w_in[:, A_HEADS * HEAD_DIM:a_cols]], axis=1)
    w_lat = w_in[:, a_cols:a_cols + B_Q_RANK + B_KV_RANK]
    w_kr = w_in[:, a_cols + B_Q_RANK + B_KV_RANK:]
    reps = LANES // B_ROPE
    w_full = jnp.concatenate(
        [w_a, w_lat, jnp.tile(w_kr, (1, reps)), jnp.tile(_rot_half_cols(w_kr, B_ROPE), (1, reps))], axis=1).astype(BF16)
    uq = w_uq.reshape(B_Q_RANK, B_HEADS, B_NOPE + B_ROPE)
    uq_n = uq[:, :, :B_NOPE].reshape(B_Q_RANK, B_HEADS * B_NOPE)
    uq_r = uq[:, :, B_NOPE:].reshape(B_Q_RANK, B_HEADS * B_ROPE)
    w_uq_full = jnp.concatenate([uq_n, uq_r, _rot_half_cols(uq_r, B_ROPE)], axis=1).astype(BF16)
    ukv = w_ukv.reshape(B_KV_RANK, B_HEADS, 2, B_NOPE)
    w_ukv_full = jnp.concatenate([ukv[:, :, 0].reshape(B_KV_RANK, -1), ukv[:, :, 1].reshape(B_KV_RANK, -1)], axis=1).astype(BF16)
    inv = ROPE_THETA ** (-jnp.arange(0, B_ROPE, 2, dtype=F32) / B_ROPE)
    ang = jnp.arange(seq, dtype=F32)[:, None] * inv[None, :]
    cos2 = jnp.concatenate([jnp.cos(ang)] * 2, axis=1)
    sin2 = jnp.concatenate([jnp.sin(ang)] * 2, axis=1)
    cq, sq = jnp.tile(cos2, (1, B_HEADS)), jnp.tile(sin2, (1, B_HEADS))
    ck, sk = jnp.tile(cos2, (1, reps)), jnp.tile(sin2, (1, reps))

    row = lambda i: (i, 0)
    per_seq = seq // tm
    pos = lambda i: (i % per_seq, 0)
    nq_n = B_HEADS * B_NOPE
    nq_r = B_HEADS * B_ROPE
    outs = [(a_cols, F32), (nq_n, BF16), (nq_r, BF16), (nq_n, BF16), (LANES, BF16), (nq_n, BF16)]
    return pl.pallas_call(
        functools.partial(_ab_in_kernel, scale=(B_NOPE + B_ROPE) ** -0.5),
        grid=(n // tm,),
        in_specs=[pl.BlockSpec((tm, d), row), _const_spec((1, d)), _const_spec(w_full.shape),
                  _const_spec((1, B_Q_RANK)), _const_spec(w_uq_full.shape),
                  _const_spec((1, B_KV_RANK)), _const_spec(w_ukv_full.shape),
                  pl.BlockSpec((tm, nq_r), pos), pl.BlockSpec((tm, nq_r), pos),
                  pl.BlockSpec((tm, LANES), pos), pl.BlockSpec((tm, LANES), pos)],
        out_specs=[pl.BlockSpec((tm, c), row) for c, _ in outs],
        out_shape=[jax.ShapeDtypeStruct((n, c), dt) for c, dt in outs],
        compiler_params=_params(("parallel",)),
        name="ab_in_proj",
    )(x, g.reshape(1, d), w_full, q_g.reshape(1, -1), w_uq_full, kv_g.reshape(1, -1), w_ukv_full, cq, sq, ck, sk)


def _low_head_lanes():
    return lax.broadcasted_iota(jnp.int32, (1, LANES), 1) < HEAD_DIM


def _stack_heads(x, lo):
    zero = jnp.zeros_like(x)
    return jnp.concatenate([jnp.where(lo, x, zero), jnp.where(lo, zero, x)], axis=0)


def _pv(p, v, lo):
    r = p.shape[0] // 2
    zero = jnp.zeros_like(v)
    return (jnp.dot(p[:r], jnp.where(lo, v, zero), preferred_element_type=F32)
            + jnp.dot(p[r:], jnp.where(lo, zero, v), preferred_element_type=F32))


def _unstack(x, lo):
    r = x.shape[0] // 2
    return jnp.where(lo, x[:r], x[r:])


def _dilated_kernel(*refs, n_lb, unroll):
    q_refs, k_refs, v_refs = refs[:n_lb], refs[n_lb:2 * n_lb], refs[2 * n_lb:3 * n_lb]
    bias_ref, o_ref, qd, kd, vd, lwd, od, lws, os_ = refs[3 * n_lb:]
    seq = o_ref.shape[0]
    lo = _low_head_lanes()

    def run_branch(bi, nb, src_q, src_k, src_v, dst_lw, dst_o):
        def blocks(it, _):
            work = []
            for u in range(unroll):
                idx = it * unroll + u
                base = pl.multiple_of(idx * BLOCK, BLOCK)
                cur = pl.ds(base, BLOCK)
                prv = pl.ds(pl.multiple_of(jnp.maximum(base - BLOCK, 0), BLOCK), BLOCK)
                pen = jnp.where((idx % nb) == 0, NEG_BIG, 0.0).astype(F32)
                work += [(lb, cur, prv, pen) for lb in range(n_lb)]
            scores = []
            for lb, cur, prv, pen in work:
                qs = _stack_heads(src_q[lb][cur, :].astype(BF16), lo)
                s_c = (lax.dot_general(qs, src_k[lb][cur, :].astype(BF16), _NT, preferred_element_type=F32)
                       + bias_ref[bi, lb, :, BLOCK:])
                s_p = None
                if nb > 1:
                    s_p = (lax.dot_general(qs, src_k[lb][prv, :].astype(BF16), _NT, preferred_element_type=F32)
                           + bias_ref[bi, lb, :, :BLOCK] + pen)
                scores.append((s_c, s_p))
            probs = []
            for s_c, s_p in scores:
                m = jnp.max(s_c if s_p is None else jnp.maximum(s_c, s_p), axis=-1, keepdims=True)
                p_c = jnp.exp(s_c - m)
                p_p = None if s_p is None else jnp.exp(s_p - m)
                l = jnp.sum(p_c if s_p is None else p_c + p_p, axis=-1, keepdims=True)
                probs.append((m, l, p_c, p_p))
            for (lb, cur, prv, _), (m, l, p_c, p_p) in zip(work, probs):
                acc = _pv(p_c.astype(BF16), src_v[lb][cur, :].astype(BF16), lo)
                if nb > 1:
                    acc = acc + _pv(p_p.astype(BF16), src_v[lb][prv, :].astype(BF16), lo)
                dst_o[lb][cur, :] = acc * _unstack(1.0 / l, lo)
                dst_lw[lb][cur, :] = _unstack(m + jnp.log(l), lo)
            return 0

        lax.fori_loop(0, seq // BLOCK // unroll, blocks, 0)

    per_lb = lambda ref, *lead: [ref.at[(*lead, lb)] for lb in range(n_lb)]
    for bi, (_, dil) in enumerate(A_BRANCHES):
        sub = seq // dil
        nb = -(-sub // BLOCK)
        if dil == 1:
            run_branch(bi, nb, q_refs, k_refs, v_refs, per_lb(lws, bi), per_lb(os_, bi))
            continue
        for lb in range(n_lb):
            for r in range(dil):
                rows = pl.ds(r, sub, stride=dil)
                qd[lb, r * sub:(r + 1) * sub, :] = q_refs[lb][rows, :].astype(BF16)
                kd[lb, r * sub:(r + 1) * sub, :] = k_refs[lb][rows, :].astype(BF16)
                vd[lb, r * sub:(r + 1) * sub, :] = v_refs[lb][rows, :].astype(BF16)
        run_branch(bi, nb, per_lb(qd), per_lb(kd), per_lb(vd), per_lb(lwd), per_lb(od))
        for lb in range(n_lb):
            for r in range(dil):
                rows = pl.ds(r, sub, stride=dil)
                lws[bi, lb, rows, :] = lwd[lb, r * sub:(r + 1) * sub, :]
                os_[bi, lb, rows, :] = od[lb, r * sub:(r + 1) * sub, :]

    nbr = len(A_BRANCHES)

    def merge(idx, _):
        rows = pl.ds(pl.multiple_of(idx * BLOCK, BLOCK), BLOCK)
        for lb in range(n_lb):
            lw = [lws[i, lb, rows, :] for i in range(nbr)]
            top = functools.reduce(jnp.maximum, lw)
            w = [jnp.exp(x - top) for x in lw]
            num = sum(w[i] * os_[i, lb, rows, :] for i in range(nbr))
            o_ref[rows, lb * LANES:(lb + 1) * LANES] = (num / sum(w)).astype(BF16)
        return 0

    lax.fori_loop(0, seq // BLOCK, merge, 0)


def _alibi_bias():
    slopes = 2.0 ** (-8.0 * np.arange(1, A_HEADS + 1, dtype=np.float64) / A_HEADS)
    qi = np.arange(BLOCK)[:, None]
    kk = np.arange(2 * BLOCK)[None, :]
    rel = qi + BLOCK - kk
    out = np.empty((len(A_BRANCHES), A_HEADS, BLOCK, 2 * BLOCK), np.float32)
    for bi, (window, dil) in enumerate(A_BRANCHES):
        valid = (rel >= 0) & (rel <= window // dil)
        bias = -slopes[:, None, None] * (rel * dil)[None]
        out[bi] = np.where(valid[None], bias, NEG_BIG)
    return jnp.asarray(out.reshape(len(A_BRANCHES), A_HEADS // 2, 2 * BLOCK, 2 * BLOCK))


def _dilated_attn(a_qkv, batch, seq, *, n_lb=2, unroll=2):
    a3 = a_qkv.reshape(batch, seq, a_qkv.shape[1])
    width = n_lb * LANES
    groups = A_HEADS * HEAD_DIM // width
    bias = _alibi_bias()
    blk = lambda t, lb: pl.BlockSpec((None, seq, LANES), lambda g, b: (b, 0, (t * groups + g) * n_lb + lb))
    nbr = len(A_BRANCHES)
    out = pl.pallas_call(
        functools.partial(_dilated_kernel, n_lb=n_lb, unroll=unroll),
        grid=(groups, batch),
        in_specs=[blk(t, lb) for t in range(3) for lb in range(n_lb)]
                 + [pl.BlockSpec((nbr, n_lb, 2 * BLOCK, 2 * BLOCK), lambda g, b: (0, g, 0, 0))],
        out_specs=pl.BlockSpec((None, seq, width), lambda g, b: (b, 0, g)),
        out_shape=jax.ShapeDtypeStruct((batch, seq, groups * width), BF16),
        scratch_shapes=[pltpu.VMEM((n_lb, seq, LANES), BF16)] * 3 + [pltpu.VMEM((n_lb, seq, LANES), F32)] * 2
                       + [pltpu.VMEM((nbr, n_lb, seq, LANES), F32)] * 2,
        compiler_params=_params(("parallel", "parallel")),
        name="dilated_attn",
    )(*([a3] * (3 * n_lb)), bias)
    return out.reshape(batch * seq, groups * width)


def _mla_kernel(qn_ref, qr_ref, kn_ref, kr_ref, v_ref, o_ref, qs_ref, *, tq, rc):
    i = pl.program_id(2)
    n_lb = qn_ref.shape[1] // LANES
    n_rc = tq // rc
    lane = lax.broadcasted_iota(jnp.int32, (1, LANES), 1)
    lo = lane < HEAD_DIM
    row = lax.broadcasted_iota(jnp.int32, (rc, tq), 0)
    col = lax.broadcasted_iota(jnp.int32, (rc, tq), 1)

    qr = qr_ref[...]
    for lb in range(n_lb):
        qn = qn_ref[:, lb * LANES:(lb + 1) * LANES]
        for e in range(2):
            slot = 2 * lb + e
            in_slot = (lane >= slot * B_ROPE) & (lane < (slot + 1) * B_ROPE)
            qs_ref[lb, e, :, :LANES] = jnp.where(lo if e == 0 else ~lo, qn, jnp.zeros_like(qn))
            qs_ref[lb, e, :, LANES:] = jnp.where(in_slot, qr, jnp.zeros_like(qr))

    chains = [(lb, c, e) for lb in range(n_lb) for c in range(n_rc) for e in range(2)]

    def step(kb, carry, masked):
        rows = pl.ds(pl.multiple_of(kb * tq, tq), tq)
        kr = kr_ref[rows, :]
        ks = [jnp.concatenate([kn_ref[rows, lb * LANES:(lb + 1) * LANES], kr], axis=1) for lb in range(n_lb)]
        ss = [lax.dot_general(qs_ref[lb, e, c * rc:(c + 1) * rc, :], ks[lb], _NT, preferred_element_type=F32)
              for lb, c, e in chains]
        new_m, new_l, alphas, ps = [], [], [], []
        for n, ((lb, c, e), s) in enumerate(zip(chains, ss)):
            if masked:
                s = jnp.where(col <= row + c * rc, s, NEG_BIG)
            m_new = jnp.maximum(carry[0][n], jnp.max(s, axis=-1, keepdims=True))
            alpha = jnp.exp(carry[0][n] - m_new)
            p = jnp.exp(s - m_new)
            new_m.append(m_new)
            new_l.append(alpha * carry[1][n] + jnp.sum(p, axis=-1, keepdims=True))
            alphas.append(alpha)
            ps.append(p.astype(BF16))
        new_acc = []
        for lb in range(n_lb):
            v = v_ref[rows, lb * LANES:(lb + 1) * LANES]
            zero = jnp.zeros_like(v)
            vs = (jnp.where(lo, v, zero), jnp.where(lo, zero, v))
            for c in range(n_rc):
                n = (lb * n_rc + c) * 2
                new_acc.append(jnp.where(lo, alphas[n], alphas[n + 1]) * carry[2][lb * n_rc + c]
                               + jnp.dot(ps[n], vs[0], preferred_element_type=F32)
                               + jnp.dot(ps[n + 1], vs[1], preferred_element_type=F32))
        return tuple(new_m), tuple(new_l), tuple(new_acc)

    init = (tuple(jnp.full((rc, 1), NEG_BIG, F32) for _ in chains),
            tuple(jnp.zeros((rc, 1), F32) for _ in chains),
            tuple(jnp.zeros((rc, LANES), F32) for _ in range(len(chains) // 2)))
    carry = lax.fori_loop(0, i, functools.partial(step, masked=False), init)
    _, l, acc = step(i, carry, True)
    for lb in range(n_lb):
        for c in range(n_rc):
            n = lb * n_rc + c
            o_ref[c * rc:(c + 1) * rc, lb * LANES:(lb + 1) * LANES] = (
                acc[n] / jnp.where(lo, l[2 * n], l[2 * n + 1])).astype(BF16)


def _mla_attn(qn, qr, kn, kr, v, batch, seq, *, tq=256, rc=128):
    width = (LANES // B_ROPE) * B_NOPE
    groups = qn.shape[1] // width
    r3 = lambda t: t.reshape(batch, seq, t.shape[1])
    out = pl.pallas_call(
        functools.partial(_mla_kernel, tq=tq, rc=rc),
        grid=(batch, groups, seq // tq),
        in_specs=[pl.BlockSpec((None, tq, width), lambda b, g, i: (b, i, g)),
                  pl.BlockSpec((None, tq, LANES), lambda b, g, i: (b, i, g)),
                  pl.BlockSpec((None, seq, width), lambda b, g, i: (b, 0, g)),
                  pl.BlockSpec((None, seq, LANES), lambda b, g, i: (b, 0, 0)),
                  pl.BlockSpec((None, seq, width), lambda b, g, i: (b, 0, g))],
        out_specs=pl.BlockSpec((None, tq, width), lambda b, g, i: (b, i, g)),
        out_shape=jax.ShapeDtypeStruct((batch, seq, groups * width), BF16),
        scratch_shapes=[pltpu.VMEM((width // LANES, 2, tq, 2 * LANES), BF16)],
        compiler_params=_params(("parallel", "parallel", "arbitrary")),
        name="mla_attn",
    )(r3(qn), r3(qr), r3(kn), r3(kr), r3(v))
    return out.reshape(batch * seq, groups * width)


def _norm_proj_kernel(x_ref, g_ref, w_ref, o_ref):
    h = _rms(x_ref[...], g_ref[...]).astype(BF16)
    o_ref[...] = jnp.dot(h, w_ref[...], preferred_element_type=F32).astype(o_ref.dtype)


def _sb_in_proj(x, g, w_in, *, tm=512):
    n, d = x.shape
    nq = C_HEADS * HEAD_DIM
    w = jnp.concatenate([w_in[:, :nq] * HEAD_DIM ** -0.5, w_in[:, nq:]], axis=1).astype(BF16)
    return pl.pallas_call(
        _norm_proj_kernel,
        grid=(n // tm,),
        in_specs=[pl.BlockSpec((tm, d), lambda i: (i, 0)), _const_spec((1, d)), _const_spec(w.shape)],
        out_specs=pl.BlockSpec((tm, w.shape[1]), lambda i: (i, 0)),
        out_shape=jax.ShapeDtypeStruct((n, w.shape[1]), BF16),
        compiler_params=_params(("parallel",)),
        name="sb_in_proj",
    )(x, g.reshape(1, d), w)


def _neg_abs(x):
    bits = lax.bitcast_convert_type(x, jnp.uint32) | jnp.uint32(0x80000000)
    return lax.bitcast_convert_type(bits, F32)


def _sb_kernel(q_ref, k_ref, v_ref, tri_ref, o_ref, qs_ref, *, tq, rc):
    i = pl.program_id(2)
    n_lb = q_ref.shape[1] // LANES
    n_rc = tq // rc
    lo = _low_head_lanes()
    row = lax.broadcasted_iota(jnp.int32, (rc, tq), 0)
    col = lax.broadcasted_iota(jnp.int32, (rc, tq), 1)
    for lb in range(n_lb):
        x = q_ref[:, lb * LANES:(lb + 1) * LANES]
        zero = jnp.zeros_like(x)
        qs_ref[lb, 0] = jnp.where(lo, x, zero)
        qs_ref[lb, 1] = jnp.where(lo, zero, x)

    def step(kb, carry, diagonal):
        rows = pl.ds(pl.multiple_of(kb * tq, tq), tq)
        tri = tri_ref[...]
        chains = [(lb, c, e) for lb in range(n_lb) for c in range(n_rc) for e in range(2)]
        vs = []
        for lb in range(n_lb):
            v = v_ref[rows, lb * LANES:(lb + 1) * LANES]
            zero = jnp.zeros_like(v)
            vs.append((jnp.where(lo, v, zero), jnp.where(lo, zero, v)))
        zs = [lax.dot_general(qs_ref[lb, e, c * rc:(c + 1) * rc, :], k_ref[rows, lb * LANES:(lb + 1) * LANES],
                              _NT, preferred_element_type=F32) for lb, c, e in chains]
        incls = []
        for (lb, c, e), z in zip(chains, zs):
            sp = jnp.maximum(z, 0.0) + jnp.log(1.0 + jnp.exp(_neg_abs(z)))
            if diagonal:
                sp = jnp.where(col < row + c * rc, sp, 0.0)
            incls.append(jnp.dot(sp.astype(BF16), tri, preferred_element_type=F32))
        new_done, des = [], []
        for n, ((lb, c, e), z, incl) in enumerate(zip(chains, zs, incls)):
            a = jnp.exp(z - incl)
            if diagonal:
                a = jnp.where(col < row + c * rc, a, 0.0)
            de = jnp.dot(a.astype(BF16), vs[lb][e], preferred_element_type=F32)
            total = jnp.broadcast_to(incl[:, 0:1], (rc, LANES))
            if diagonal:
                new_done.append(total)
            else:
                de = de * jnp.exp(-carry[0][n])
                new_done.append(carry[0][n] + total)
            des.append(de)
        new_acc = [des[2 * n] + des[2 * n + 1] for n in range(len(chains) // 2)]
        if not diagonal:
            new_acc = [carry[1][n] + new_acc[n] for n in range(len(new_acc))]
        return tuple(new_done), tuple(new_acc)

    carry = step(i, None, True)
    _, acc = lax.fori_loop(0, i, lambda t, c: step(i - 1 - t, c, False), carry)
    for lb in range(n_lb):
        for c in range(n_rc):
            o_ref[c * rc:(c + 1) * rc, lb * LANES:(lb + 1) * LANES] = acc[lb * n_rc + c].astype(BF16)


def _sb_attn(qkv, batch, seq, *, tq=256, n_lb=2, rc=128):
    width = n_lb * LANES
    groups = C_HEADS * HEAD_DIM // width
    q3 = qkv.reshape(batch, seq, qkv.shape[1])
    tri = jnp.asarray(np.tril(np.ones((tq, tq), np.float32)), BF16)
    out = pl.pallas_call(
        functools.partial(_sb_kernel, tq=tq, rc=rc),
        grid=(batch, groups, seq // tq),
        in_specs=[pl.BlockSpec((None, tq, width), lambda b, g, i: (b, i, g)),
                  pl.BlockSpec((None, seq, width), lambda b, g, i: (b, 0, groups + g)),
                  pl.BlockSpec((None, seq, width), lambda b, g, i: (b, 0, 2 * groups + g)),
                  pl.BlockSpec((tq, tq), lambda b, g, i: (0, 0))],
        out_specs=pl.BlockSpec((None, tq, width), lambda b, g, i: (b, i, g)),
        out_shape=jax.ShapeDtypeStruct((batch, seq, groups * width), BF16),
        scratch_shapes=[pltpu.VMEM((n_lb, 2, tq, LANES), BF16)],
        compiler_params=_params(("parallel", "parallel", "arbitrary")),
        name="sb_attn",
    )(q3, q3, q3, tri)
    return out.reshape(batch * seq, groups * width)


def kernel(x, ffn_norm_g, mix_norm_g, ffn_w_gate, ffn_w_up, ffn_w_down, ab_w_in, mla_q_norm_g,
           mla_w_uq, mla_kv_norm_g, mla_w_ukv, ab_w_out, sb_w_in, sb_w_out, final_norm_g):
    batch, seq, d = x.shape
    depth = ffn_norm_g.shape[0]
    h = x.reshape(batch * seq, d)
    bf = lambda w: w.astype(BF16)

    def ffn(h, pre, i, s, final_g=None):
        return _ffn(h, pre, ffn_norm_g[i, s], bf(ffn_w_gate[i, s]), bf(ffn_w_up[i, s]), bf(ffn_w_down[i, s]), final_g)

    for i in range(depth):
        h = ffn(h, [], i, 0)
        if i % 2 == 0:
            e = i // 2
            a_qkv, qn, qr, kn, kr, v = _ab_in_proj(h, seq, mix_norm_g[i], ab_w_in[e], mla_q_norm_g[e],
                                                   mla_w_uq[e], mla_kv_norm_g[e], mla_w_ukv[e])
            o_a = _dilated_attn(a_qkv, batch, seq)
            o_b = _mla_attn(qn, qr, kn, kr, v, batch, seq)
            w_out = bf(ab_w_out[e])
            na = o_a.shape[1]
            pre = [(o_a, w_out[:na]), (o_b, w_out[na:])]
        else:
            o = i // 2
            qkv = _sb_in_proj(h, mix_norm_g[i], sb_w_in[o])
            pre = [(_sb_attn(qkv, batch, seq), bf(sb_w_out[o]))]
        h = ffn(h, pre, i, 1, final_norm_g if i == depth - 1 else None)
    return h.reshape(batch, seq, d)
```

```python
import functools

import numpy as np
import jax
import jax.numpy as jnp
from jax import lax
from jax.experimental import pallas as pl
from jax.experimental.pallas import tpu as pltpu

F32 = jnp.float32
BF16 = jnp.bfloat16

NORM_EPS = 1e-6
ROPE_THETA = 10000.0
LANES = 128
HEAD_DIM = 64
BLOCK = 128
A_HEADS = 8
A_BRANCHES = ((128, 1), (512, 4), (2048, 16))
B_HEADS = 8
B_NOPE = 64
B_ROPE = 32
B_Q_RANK = 256
B_KV_RANK = 128
C_HEADS = 16
NEG_BIG = -1e30
LOG2E = 1.4426950408889634
VMEM_LIMIT = 58 * 1024 * 1024

_NT = (((1,), (1,)), ((), ()))


def _rms(x, g):
    return x * lax.rsqrt(jnp.mean(x * x, axis=-1, keepdims=True) + NORM_EPS) * g


def _const_spec(shape):
    nd = len(shape)
    return pl.BlockSpec(shape, lambda *_: (0,) * nd, pipeline_mode=pl.Buffered(1))


def _params(sem):
    return pltpu.CompilerParams(dimension_semantics=sem, vmem_limit_bytes=VMEM_LIMIT)


def _ffn_kernel(*refs, n_pre, final_norm, tf):
    x_ref = refs[0]
    pre = refs[1:1 + 2 * n_pre]
    g_ref, wg_ref, wu_ref, wd_ref = refs[1 + 2 * n_pre:5 + 2 * n_pre]
    rest = refs[5 + 2 * n_pre:]
    fg_ref = rest[0] if final_norm else None
    o_ref = rest[-1]

    x = x_ref[...]
    for p in range(n_pre):
        x = x + jnp.dot(pre[2 * p][...], pre[2 * p + 1][...], preferred_element_type=F32)
    h = _rms(x, g_ref[...]).astype(BF16)
    d_ff = wg_ref.shape[1]
    acc = None
    for c in range(d_ff // tf):
        sl = slice(c * tf, (c + 1) * tf)
        gate = jnp.dot(h, wg_ref[:, sl], preferred_element_type=F32)
        up = jnp.dot(h, wu_ref[:, sl], preferred_element_type=F32)
        a = (gate * jax.nn.sigmoid(gate) * up).astype(BF16)
        d = jnp.dot(a, wd_ref[sl, :], preferred_element_type=F32)
        acc = d if acc is None else acc + d
    y = x + 0.5 * acc
    if final_norm:
        y = _rms(y, fg_ref[...])
    o_ref[...] = y


def _ffn(x, pre, g, wg, wu, wd, which, final_g=None, *, tm=512, tf=256):
    n, d = x.shape
    row = lambda i: (i, 0)
    picked = lambda w: pl.BlockSpec((None, None) + w.shape[2:], lambda i: which + (0, 0),
                                    pipeline_mode=pl.Buffered(1))
    args = [x]
    specs = [pl.BlockSpec((tm, d), row)]
    for o, w in pre:
        args += [o, w]
        specs += [pl.BlockSpec((tm, o.shape[1]), row), _const_spec(w.shape)]
    args += [g.reshape(1, d), wg, wu, wd]
    specs += [_const_spec((1, d)), picked(wg), picked(wu), picked(wd)]
    if final_g is not None:
        args.append(final_g.reshape(1, d))
        specs.append(_const_spec((1, d)))
    return pl.pallas_call(
        functools.partial(_ffn_kernel, n_pre=len(pre), final_norm=final_g is not None, tf=tf),
        grid=(n // tm,),
        in_specs=specs,
        out_specs=pl.BlockSpec((tm, d), row),
        out_shape=jax.ShapeDtypeStruct((n, d), F32),
        compiler_params=_params(("parallel",)),
        name="ffn",
    )(*args)


def _ab_in_kernel(x_ref, g_ref, win_ref, qg_ref, wuq_ref, kvg_ref, wukv_ref,
                  cq_ref, sq_ref, ck_ref, sk_ref,
                  a_ref, qn_ref, qr_ref, kn_ref, kr_ref, v_ref, *, scale):
    h = _rms(x_ref[...], g_ref[...]).astype(BF16)
    p = jnp.dot(h, win_ref[...], preferred_element_type=F32)
    na = a_ref.shape[1]
    a_ref[...] = p[:, :na]
    c_q = p[:, na:na + B_Q_RANK]
    c_kv = p[:, na + B_Q_RANK:na + B_Q_RANK + B_KV_RANK]
    k0 = na + B_Q_RANK + B_KV_RANK
    kr_ref[...] = (p[:, k0:k0 + LANES] * ck_ref[...] + p[:, k0 + LANES:k0 + 2 * LANES] * sk_ref[...]).astype(BF16)
    q = jnp.dot(_rms(c_q, qg_ref[...]).astype(BF16), wuq_ref[...], preferred_element_type=F32)
    nn = qn_ref.shape[1]
    nr = qr_ref.shape[1]
    qn_ref[...] = (q[:, :nn] * scale).astype(BF16)
    qr_ref[...] = ((q[:, nn:nn + nr] * cq_ref[...] + q[:, nn + nr:nn + 2 * nr] * sq_ref[...]) * scale).astype(BF16)
    kv = jnp.dot(_rms(c_kv, kvg_ref[...]).astype(BF16), wukv_ref[...], preferred_element_type=F32)
    kn_ref[...] = kv[:, :nn].astype(BF16)
    v_ref[...] = kv[:, nn:].astype(BF16)


def _rot_half_cols(w, width):
    k, n = w.shape
    w3 = w.reshape(k, n // width, 2, width // 2)
    return jnp.stack([-w3[:, :, 1], w3[:, :, 0]], axis=2).reshape(k, n)


def _ab_in_proj(x, seq, g, w_in, q_g, w_uq, kv_g, w_ukv, *, tm=512):
    n, d = x.shape
    a_cols = 3 * A_HEADS * HEAD_DIM
    a_scale = HEAD_DIM ** -0.5
    w_a = jnp.concatenate([w_in[:, :A_HEADS * HEAD_DIM] * a_scale, w_in[:, A_HEADS * HEAD_DIM:a_cols]], axis=1)
    w_lat = w_in[:, a_cols:a_cols + B_Q_RANK + B_KV_RANK]
    w_kr = w_in[:, a_cols + B_Q_RANK + B_KV_RANK:]
    reps = LANES // B_ROPE
    w_full = jnp.concatenate(
        [w_a, w_lat, jnp.tile(w_kr, (1, reps)), jnp.tile(_rot_half_cols(w_kr, B_ROPE), (1, reps))], axis=1).astype(BF16)
    uq = w_uq.reshape(B_Q_RANK, B_HEADS, B_NOPE + B_ROPE)
    uq_n = uq[:, :, :B_NOPE].reshape(B_Q_RANK, B_HEADS * B_NOPE)
    uq_r = uq[:, :, B_NOPE:].reshape(B_Q_RANK, B_HEADS * B_ROPE)
    w_uq_full = jnp.concatenate([uq_n, uq_r, _rot_half_cols(uq_r, B_ROPE)], axis=1).astype(BF16)
    ukv = w_ukv.reshape(B_KV_RANK, B_HEADS, 2, B_NOPE)
    w_ukv_full = jnp.concatenate([ukv[:, :, 0].reshape(B_KV_RANK, -1), ukv[:, :, 1].reshape(B_KV_RANK, -1)], axis=1).astype(BF16)
    inv = ROPE_THETA ** (-jnp.arange(0, B_ROPE, 2, dtype=F32) / B_ROPE)
    ang = jnp.arange(seq, dtype=F32)[:, None] * inv[None, :]
    cos2 = jnp.concatenate([jnp.cos(ang)] * 2, axis=1)
    sin2 = jnp.concatenate([jnp.sin(ang)] * 2, axis=1)
    cq, sq = jnp.tile(cos2, (1, B_HEADS)), jnp.tile(sin2, (1, B_HEADS))
    ck, sk = jnp.tile(cos2, (1, reps)), jnp.tile(sin2, (1, reps))

    row = lambda i: (i, 0)
    per_seq = seq // tm
    pos = lambda i: (i % per_seq, 0)
    nq_n = B_HEADS * B_NOPE
    nq_r = B_HEADS * B_ROPE
    outs = [(a_cols, F32), (nq_n, BF16), (nq_r, BF16), (nq_n, BF16), (LANES, BF16), (nq_n, BF16)]
    return pl.pallas_call(
        functools.partial(_ab_in_kernel, scale=(B_NOPE + B_ROPE) ** -0.5),
        grid=(n // tm,),
        in_specs=[pl.BlockSpec((tm, d), row), _const_spec((1, d)), _const_spec(w_full.shape),
                  _const_spec((1, B_Q_RANK)), _const_spec(w_uq_full.shape),
                  _const_spec((1, B_KV_RANK)), _const_spec(w_ukv_full.shape),
                  pl.BlockSpec((tm, nq_r), pos), pl.BlockSpec((tm, nq_r), pos),
                  pl.BlockSpec((tm, LANES), pos), pl.BlockSpec((tm, LANES), pos)],
        out_specs=[pl.BlockSpec((tm, c), row) for c, _ in outs],
        out_shape=[jax.ShapeDtypeStruct((n, c), dt) for c, dt in outs],
        compiler_params=_params(("parallel",)),
        name="ab_in_proj",
    )(x, g.reshape(1, d), w_full, q_g.reshape(1, -1), w_uq_full, kv_g.reshape(1, -1), w_ukv_full, cq, sq, ck, sk)


def _low_head_lanes():
    return lax.broadcasted_iota(jnp.int32, (1, LANES), 1) < HEAD_DIM


def _stack_heads(x, lo):
    zero = jnp.zeros_like(x)
    return jnp.concatenate([jnp.where(lo, x, zero), jnp.where(lo, zero, x)], axis=0)


def _pv(p, v, lo):
    r = p.shape[0] // 2
    zero = jnp.zeros_like(v)
    return (jnp.dot(p[:r], jnp.where(lo, v, zero), preferred_element_type=F32)
            + jnp.dot(p[r:], jnp.where(lo, zero, v), preferred_element_type=F32))


def _unstack(x, lo):
    r = x.shape[0] // 2
    return jnp.where(lo, x[:r], x[r:])


def _dilated_kernel(*refs, n_lb, unroll):
    q_refs, k_refs, v_refs = refs[:n_lb], refs[n_lb:2 * n_lb], refs[2 * n_lb:3 * n_lb]
    bias_ref, o_ref, qd, kd, vd, lwd, od, lws, os_ = refs[3 * n_lb:]
    seq = o_ref.shape[0]
    lo = _low_head_lanes()

    def run_branch(bi, nb, src_q, src_k, src_v, dst_lw, dst_o):
        def blocks(it, _):
            work = []
            for u in range(unroll):
                idx = it * unroll + u
                base = pl.multiple_of(idx * BLOCK, BLOCK)
                cur = pl.ds(base, BLOCK)
                prv = pl.ds(pl.multiple_of(jnp.maximum(base - BLOCK, 0), BLOCK), BLOCK)
                pen = jnp.where((idx % nb) == 0, NEG_BIG, 0.0).astype(F32)
                work += [(lb, cur, prv, pen) for lb in range(n_lb)]
            scores = []
            for lb, cur, prv, pen in work:
                qs = _stack_heads(src_q[lb][cur, :].astype(BF16), lo)
                s_c = (lax.dot_general(qs, src_k[lb][cur, :].astype(BF16), _NT, preferred_element_type=F32)
                       + bias_ref[bi, lb, :, BLOCK:])
                s_p = None
                if nb > 1:
                    s_p = (lax.dot_general(qs, src_k[lb][prv, :].astype(BF16), _NT, preferred_element_type=F32)
                           + bias_ref[bi, lb, :, :BLOCK] + pen)
                scores.append((s_c, s_p))
            probs = []
            for s_c, s_p in scores:
                m = jnp.max(s_c if s_p is None else jnp.maximum(s_c, s_p), axis=-1, keepdims=True)
                p_c = jnp.exp(s_c - m)
                p_p = None if s_p is None else jnp.exp(s_p - m)
                l = jnp.sum(p_c if s_p is None else p_c + p_p, axis=-1, keepdims=True)
                probs.append((m, l, p_c, p_p))
            for (lb, cur, prv, _), (m, l, p_c, p_p) in zip(work, probs):
                acc = _pv(p_c.astype(BF16), src_v[lb][cur, :].astype(BF16), lo)
                if nb > 1:
                    acc = acc + _pv(p_p.astype(BF16), src_v[lb][prv, :].astype(BF16), lo)
                dst_o[lb][cur, :] = acc * _unstack(1.0 / l, lo)
                dst_lw[lb][cur, :] = _unstack(m + jnp.log(l), lo)
            return 0

        lax.fori_loop(0, seq // BLOCK // unroll, blocks, 0)

    per_lb = lambda ref, *lead: [ref.at[(*lead, lb)] for lb in range(n_lb)]
    for bi, (_, dil) in enumerate(A_BRANCHES):
        sub = seq // dil
        nb = -(-sub // BLOCK)
        if dil == 1:
            run_branch(bi, nb, q_refs, k_refs, v_refs, per_lb(lws, bi), per_lb(os_, bi))
            continue
        for lb in range(n_lb):
            for r in range(dil):
                rows = pl.ds(r, sub, stride=dil)
                qd[lb, r * sub:(r + 1) * sub, :] = q_refs[lb][rows, :].astype(BF16)
                kd[lb, r * sub:(r + 1) * sub, :] = k_refs[lb][rows, :].astype(BF16)
                vd[lb, r * sub:(r + 1) * sub, :] = v_refs[lb][rows, :].astype(BF16)
        run_branch(bi, nb, per_lb(qd), per_lb(kd), per_lb(vd), per_lb(lwd), per_lb(od))
        for lb in range(n_lb):
            for r in range(dil):
                rows = pl.ds(r, sub, stride=dil)
                lws[bi, lb, rows, :] = lwd[lb, r * sub:(r + 1) * sub, :]
                os_[bi, lb, rows, :] = od[lb, r * sub:(r + 1) * sub, :]

    nbr = len(A_BRANCHES)

    def merge(idx, _):
        rows = pl.ds(pl.multiple_of(idx * BLOCK, BLOCK), BLOCK)
        for lb in range(n_lb):
            lw = [lws[i, lb, rows, :] for i in range(nbr)]
            top = functools.reduce(jnp.maximum, lw)
            w = [jnp.exp(x - top) for x in lw]
            num = sum(w[i] * os_[i, lb, rows, :] for i in range(nbr))
            o_ref[rows, lb * LANES:(lb + 1) * LANES] = (num / sum(w)).astype(BF16)
        return 0

    lax.fori_loop(0, seq // BLOCK, merge, 0)


def _alibi_bias():
    slopes = 2.0 ** (-8.0 * np.arange(1, A_HEADS + 1, dtype=np.float64) / A_HEADS)
    qi = np.arange(BLOCK)[:, None]
    kk = np.arange(2 * BLOCK)[None, :]
    rel = qi + BLOCK - kk
    out = np.empty((len(A_BRANCHES), A_HEADS, BLOCK, 2 * BLOCK), np.float32)
    for bi, (window, dil) in enumerate(A_BRANCHES):
        valid = (rel >= 0) & (rel <= window // dil)
        bias = -slopes[:, None, None] * (rel * dil)[None]
        out[bi] = np.where(valid[None], bias, NEG_BIG)
    return jnp.asarray(out.reshape(len(A_BRANCHES), A_HEADS // 2, 2 * BLOCK, 2 * BLOCK))


def _dilated_attn(a_qkv, batch, seq, *, n_lb=2, unroll=2):
    a3 = a_qkv.reshape(batch, seq, a_qkv.shape[1])
    width = n_lb * LANES
    groups = A_HEADS * HEAD_DIM // width
    bias = _alibi_bias()
    blk = lambda t, lb: pl.BlockSpec((None, seq, LANES), lambda g, b: (b, 0, (t * groups + g) * n_lb + lb))
    nbr = len(A_BRANCHES)
    out = pl.pallas_call(
        functools.partial(_dilated_kernel, n_lb=n_lb, unroll=unroll),
        grid=(groups, batch),
        in_specs=[blk(t, lb) for t in range(3) for lb in range(n_lb)]
                 + [pl.BlockSpec((nbr, n_lb, 2 * BLOCK, 2 * BLOCK), lambda g, b: (0, g, 0, 0))],
        out_specs=pl.BlockSpec((None, seq, width), lambda g, b: (b, 0, g)),
        out_shape=jax.ShapeDtypeStruct((batch, seq, groups * width), BF16),
        scratch_shapes=[pltpu.VMEM((n_lb, seq, LANES), BF16)] * 3 + [pltpu.VMEM((n_lb, seq, LANES), F32)] * 2
                       + [pltpu.VMEM((nbr, n_lb, seq, LANES), F32)] * 2,
        compiler_params=_params(("parallel", "parallel")),
        name="dilated_attn",
    )(*([a3] * (3 * n_lb)), bias)
    return out.reshape(batch * seq, groups * width)


def _mla_kernel(qn_ref, qr_ref, kn_ref, kr_ref, v_ref, o_ref, qs_ref, *, tq, tk, rc):
    i = pl.program_id(2)
    n_lb = qn_ref.shape[1] // LANES
    n_rc = tq // rc
    lane = lax.broadcasted_iota(jnp.int32, (1, LANES), 1)
    lo = lane < HEAD_DIM
    rel = lambda w: lax.broadcasted_iota(jnp.int32, (rc, w), 1) - lax.broadcasted_iota(jnp.int32, (rc, w), 0)

    qr = qr_ref[...]
    for lb in range(n_lb):
        qn = qn_ref[:, lb * LANES:(lb + 1) * LANES]
        for e in range(2):
            slot = 2 * lb + e
            in_slot = (lane >= slot * B_ROPE) & (lane < (slot + 1) * B_ROPE)
            qs_ref[lb, e, :, :LANES] = jnp.where(lo if e == 0 else ~lo, qn, jnp.zeros_like(qn))
            qs_ref[lb, e, :, LANES:] = jnp.where(in_slot, qr, jnp.zeros_like(qr))

    chains = [(lb, c, e) for lb in range(n_lb) for c in range(n_rc) for e in range(2)]

    def step(kb, carry, diagonal):
        base = pl.multiple_of(kb * (tq if diagonal else tk), tk)
        width = lambda c: (c + 1) * rc if diagonal else tk
        ss = []
        for lb, c, e in chains:
            rows = pl.ds(base, width(c))
            k = jnp.concatenate([kn_ref[rows, lb * LANES:(lb + 1) * LANES], kr_ref[rows, :]], axis=1)
            ss.append(lax.dot_general(qs_ref[lb, e, c * rc:(c + 1) * rc, :], k, _NT, preferred_element_type=F32))
        new_m, alphas, ps = [], [], []
        for n, ((lb, c, e), s) in enumerate(zip(chains, ss)):
            if diagonal:
                s = jnp.where(rel(width(c)) <= c * rc, s, NEG_BIG)
            m_new = jnp.maximum(carry[0][n], jnp.max(s, axis=-1, keepdims=True))
            alphas.append(jnp.exp(carry[0][n] - m_new))
            ps.append(jnp.exp(s - m_new).astype(BF16))
            new_m.append(m_new)
        new_acc = []
        for n, (lb, c, e) in enumerate(chains):
            v = v_ref[pl.ds(base, width(c)), lb * LANES:(lb + 1) * LANES]
            ve = jnp.where(lo if e == 0 else ~lo, v, jnp.ones_like(v))
            new_acc.append(alphas[n] * carry[1][n] + jnp.dot(ps[n], ve, preferred_element_type=F32))
        return tuple(new_m), tuple(new_acc)

    init = (tuple(jnp.full((rc, 1), NEG_BIG, F32) for _ in chains),
            tuple(jnp.zeros((rc, LANES), F32) for _ in chains))
    carry = lax.fori_loop(0, i * (tq // tk), functools.partial(step, diagonal=False), init)
    _, acc = step(i, carry, True)
    for lb in range(n_lb):
        for c in range(n_rc):
            n = (lb * n_rc + c) * 2
            swap = lambda x: jnp.concatenate([x[:, HEAD_DIM:], x[:, :HEAD_DIM]], axis=1)
            out = [acc[n + e] / swap(acc[n + e]) for e in range(2)]
            o_ref[c * rc:(c + 1) * rc, lb * LANES:(lb + 1) * LANES] = jnp.where(lo, out[0], out[1]).astype(BF16)


def _mla_attn(qn, qr, kn, kr, v, batch, seq, *, tq=512, tk=256, rc=128):
    width = (LANES // B_ROPE) * B_NOPE
    groups = qn.shape[1] // width
    r3 = lambda t: t.reshape(batch, seq, t.shape[1])
    out = pl.pallas_call(
        functools.partial(_mla_kernel, tq=tq, tk=tk, rc=rc),
        grid=(batch, groups, seq // tq),
        in_specs=[pl.BlockSpec((None, tq, width), lambda b, g, i: (b, i, g)),
                  pl.BlockSpec((None, tq, LANES), lambda b, g, i: (b, i, g)),
                  pl.BlockSpec((None, seq, width), lambda b, g, i: (b, 0, g)),
                  pl.BlockSpec((None, seq, LANES), lambda b, g, i: (b, 0, 0)),
                  pl.BlockSpec((None, seq, width), lambda b, g, i: (b, 0, g))],
        out_specs=pl.BlockSpec((None, tq, width), lambda b, g, i: (b, i, g)),
        out_shape=jax.ShapeDtypeStruct((batch, seq, groups * width), BF16),
        scratch_shapes=[pltpu.VMEM((width // LANES, 2, tq, 2 * LANES), BF16)],
        compiler_params=_params(("parallel", "parallel", "arbitrary")),
        name="mla_attn",
    )(r3(qn), r3(qr), r3(kn), r3(kr), r3(v))
    return out.reshape(batch * seq, groups * width)


def _norm_proj_kernel(x_ref, g_ref, w_ref, o_ref):
    h = _rms(x_ref[...], g_ref[...]).astype(BF16)
    o_ref[...] = jnp.dot(h, w_ref[...], preferred_element_type=F32).astype(o_ref.dtype)


def _sb_in_proj(x, g, w_in, *, tm=512):
    n, d = x.shape
    nq = C_HEADS * HEAD_DIM
    w = jnp.concatenate([w_in[:, :nq] * HEAD_DIM ** -0.5, w_in[:, nq:]], axis=1).astype(BF16)
    return pl.pallas_call(
        _norm_proj_kernel,
        grid=(n // tm,),
        in_specs=[pl.BlockSpec((tm, d), lambda i: (i, 0)), _const_spec((1, d)), _const_spec(w.shape)],
        out_specs=pl.BlockSpec((tm, w.shape[1]), lambda i: (i, 0)),
        out_shape=jax.ShapeDtypeStruct((n, w.shape[1]), BF16),
        compiler_params=_params(("parallel",)),
        name="sb_in_proj",
    )(x, g.reshape(1, d), w)


def _sb_kernel(q_ref, k_ref, v_ref, tri_ref, o_ref, qs_ref, *, tq, tk, rc):
    i = pl.program_id(2)
    n_lb = q_ref.shape[1] // LANES
    n_rc = tq // rc
    lo = _low_head_lanes()
    rel = lambda w: lax.broadcasted_iota(jnp.int32, (rc, w), 1) - lax.broadcasted_iota(jnp.int32, (rc, w), 0)
    for lb in range(n_lb):
        x = q_ref[:, lb * LANES:(lb + 1) * LANES]
        zero = jnp.zeros_like(x)
        qs_ref[lb, 0] = jnp.where(lo, x, zero)
        qs_ref[lb, 1] = jnp.where(lo, zero, x)

    chains = [(lb, c, e) for lb in range(n_lb) for c in range(n_rc) for e in range(2)]

    def step(kb, carry, diagonal):
        base = pl.multiple_of(kb * (tq if diagonal else tk), tk)
        width = lambda c: (c + 1) * rc if diagonal else tk
        strict = lambda c: rel(width(c)) < c * rc
        zs = [lax.dot_general(qs_ref[lb, e, c * rc:(c + 1) * rc, :],
                              k_ref[pl.ds(base, width(c)), lb * LANES:(lb + 1) * LANES], _NT,
                              preferred_element_type=F32) for lb, c, e in chains]
        incls = []
        for (lb, c, e), z in zip(chains, zs):
            sp = jnp.maximum(z, 0.0) + jnp.log(1.0 + jnp.exp(-jnp.abs(z)))
            if diagonal:
                sp = jnp.where(strict(c), sp, 0.0)
            tri = tri_ref[:width(c), :width(c)]
            incls.append(jnp.dot(sp.astype(BF16), tri, preferred_element_type=F32))
        new_done, des = [], []
        for n, ((lb, c, e), z, incl) in enumerate(zip(chains, zs, incls)):
            a = jnp.exp(z - incl)
            if diagonal:
                a = jnp.where(strict(c), a, 0.0)
            v = v_ref[pl.ds(base, width(c)), lb * LANES:(lb + 1) * LANES]
            ve = jnp.where(lo if e == 0 else ~lo, v, jnp.zeros_like(v))
            de = jnp.dot(a.astype(BF16), ve, preferred_element_type=F32)
            total = jnp.broadcast_to(incl[:, 0:1], (rc, LANES))
            if diagonal:
                new_done.append(total)
            else:
                de = de * jnp.exp(-carry[0][n])
                new_done.append(carry[0][n] + total)
            des.append(de)
        new_acc = [des[2 * n] + des[2 * n + 1] for n in range(len(chains) // 2)]
        if not diagonal:
            new_acc = [carry[1][n] + new_acc[n] for n in range(len(new_acc))]
        return tuple(new_done), tuple(new_acc)

    carry = step(i, None, True)
    n_below = i * (tq // tk)
    _, acc = lax.fori_loop(0, n_below, lambda t, c: step(n_below - 1 - t, c, False), carry)
    for lb in range(n_lb):
        for c in range(n_rc):
            o_ref[c * rc:(c + 1) * rc, lb * LANES:(lb + 1) * LANES] = acc[lb * n_rc + c].astype(BF16)


def _sb_attn(qkv, batch, seq, *, tq=512, tk=256, n_lb=2, rc=128):
    width = n_lb * LANES
    groups = C_HEADS * HEAD_DIM // width
    q3 = qkv.reshape(batch, seq, qkv.shape[1])
    tri = jnp.asarray(np.tril(np.ones((tq, tq), np.float32)), BF16)
    out = pl.pallas_call(
        functools.partial(_sb_kernel, tq=tq, tk=tk, rc=rc),
        grid=(batch, groups, seq // tq),
        in_specs=[pl.BlockSpec((None, tq, width), lambda b, g, i: (b, i, g)),
                  pl.BlockSpec((None, seq, width), lambda b, g, i: (b, 0, groups + g)),
                  pl.BlockSpec((None, seq, width), lambda b, g, i: (b, 0, 2 * groups + g)),
                  pl.BlockSpec((tq, tq), lambda b, g, i: (0, 0))],
        out_specs=pl.BlockSpec((None, tq, width), lambda b, g, i: (b, i, g)),
        out_shape=jax.ShapeDtypeStruct((batch, seq, groups * width), BF16),
        scratch_shapes=[pltpu.VMEM((n_lb, 2, tq, LANES), BF16)],
        compiler_params=_params(("parallel", "parallel", "arbitrary")),
        name="sb_attn",
    )(q3, q3, q3, tri)
    return out.reshape(batch * seq, groups * width)


def kernel(x, ffn_norm_g, mix_norm_g, ffn_w_gate, ffn_w_up, ffn_w_down, ab_w_in, mla_q_norm_g,
           mla_w_uq, mla_kv_norm_g, mla_w_ukv, ab_w_out, sb_w_in, sb_w_out, final_norm_g):
    batch, seq, d = x.shape
    depth = ffn_norm_g.shape[0]
    h = x.reshape(batch * seq, d)
    bf = lambda w: w.astype(BF16)

    wg, wu, wd = bf(ffn_w_gate), bf(ffn_w_up), bf(ffn_w_down)

    def ffn(h, pre, i, s, final_g=None):
        return _ffn(h, pre, ffn_norm_g[i, s], wg, wu, wd, (i, s), final_g)

    for i in range(depth):
        h = ffn(h, [], i, 0)
        if i % 2 == 0:
            e = i // 2
            a_qkv, qn, qr, kn, kr, v = _ab_in_proj(h, seq, mix_norm_g[i], ab_w_in[e], mla_q_norm_g[e],
                                                   mla_w_uq[e], mla_kv_norm_g[e], mla_w_ukv[e])
            o_a = _dilated_attn(a_qkv, batch, seq)
            o_b = _mla_attn(qn, qr, kn, kr, v, batch, seq)
            w_out = bf(ab_w_out[e])
            na = o_a.shape[1]
            pre = [(o_a, w_out[:na]), (o_b, w_out[na:])]
        else:
            o = i // 2
            qkv = _sb_in_proj(h, mix_norm_g[i], sb_w_in[o])
            pre = [(_sb_attn(qkv, batch, seq), bf(sb_w_out[o]))]
        h = ffn(h, pre, i, 1, final_norm_g if i == depth - 1 else None)
    return h.reshape(batch, seq, d)
```

```python
import functools

import numpy as np
import jax
import jax.numpy as jnp
from jax import lax
from jax.experimental import pallas as pl
from jax.experimental.pallas import tpu as pltpu

F32 = jnp.float32
BF16 = jnp.bfloat16

NORM_EPS = 1e-6
ROPE_THETA = 10000.0
LANES = 128
HEAD_DIM = 64
BLOCK = 128
A_HEADS = 8
A_BRANCHES = ((128, 1), (512, 4), (2048, 16))
B_HEADS = 8
B_NOPE = 64
B_ROPE = 32
B_Q_RANK = 256
B_KV_RANK = 128
C_HEADS = 16
NEG_BIG = -1e30
LOG2E = 1.4426950408889634
VMEM_LIMIT = 58 * 1024 * 1024

_NT = (((1,), (1,)), ((), ()))


def _rms(x, g):
    return x * lax.rsqrt(jnp.mean(x * x, axis=-1, keepdims=True) + NORM_EPS) * g


def _const_spec(shape):
    nd = len(shape)
    return pl.BlockSpec(shape, lambda *_: (0,) * nd, pipeline_mode=pl.Buffered(1))


def _params(sem):
    return pltpu.CompilerParams(dimension_semantics=sem, vmem_limit_bytes=VMEM_LIMIT)


def _ffn_kernel(*refs, n_pre, final_norm, tf):
    x_ref = refs[0]
    pre = refs[1:1 + 2 * n_pre]
    g_ref, wg_ref, wu_ref, wd_ref = refs[1 + 2 * n_pre:5 + 2 * n_pre]
    rest = refs[5 + 2 * n_pre:]
    fg_ref = rest[0] if final_norm else None
    o_ref = rest[-1]

    x = x_ref[...]
    for p in range(n_pre):
        x = x + jnp.dot(pre[2 * p][...], pre[2 * p + 1][...], preferred_element_type=F32)
    h = _rms(x, g_ref[...]).astype(BF16)
    d_ff = wg_ref.shape[1]
    acc = None
    for c in range(d_ff // tf):
        sl = slice(c * tf, (c + 1) * tf)
        gate = jnp.dot(h, wg_ref[:, sl], preferred_element_type=F32)
        up = jnp.dot(h, wu_ref[:, sl], preferred_element_type=F32)
        a = (gate * jax.nn.sigmoid(gate) * up).astype(BF16)
        d = jnp.dot(a, wd_ref[sl, :], preferred_element_type=F32)
        acc = d if acc is None else acc + d
    y = x + 0.5 * acc
    if final_norm:
        y = _rms(y, fg_ref[...])
    o_ref[...] = y


def _ffn(x, pre, g, wg, wu, wd, which, final_g=None, *, tm=512, tf=256):
    n, d = x.shape
    row = lambda i: (i, 0)
    picked = lambda w: pl.BlockSpec((None, None) + w.shape[2:], lambda i: which + (0, 0),
                                    pipeline_mode=pl.Buffered(1))
    args = [x]
    specs = [pl.BlockSpec((tm, d), row)]
    for o, w in pre:
        args += [o, w]
        specs += [pl.BlockSpec((tm, o.shape[1]), row), _const_spec(w.shape)]
    args += [g.reshape(1, d), wg, wu, wd]
    specs += [_const_spec((1, d)), picked(wg), picked(wu), picked(wd)]
    if final_g is not None:
        args.append(final_g.reshape(1, d))
        specs.append(_const_spec((1, d)))
    return pl.pallas_call(
        functools.partial(_ffn_kernel, n_pre=len(pre), final_norm=final_g is not None, tf=tf),
        grid=(n // tm,),
        in_specs=specs,
        out_specs=pl.BlockSpec((tm, d), row),
        out_shape=jax.ShapeDtypeStruct((n, d), F32),
        compiler_params=_params(("parallel",)),
        name="ffn",
    )(*args)


def _ab_in_kernel(x_ref, g_ref, win_ref, qg_ref, wuq_ref, kvg_ref, wukv_ref,
                  cq_ref, sq_ref, ck_ref, sk_ref,
                  a_ref, qn_ref, qr_ref, kn_ref, kr_ref, v_ref, *, scale):
    h = _rms(x_ref[...], g_ref[...]).astype(BF16)
    p = jnp.dot(h, win_ref[...], preferred_element_type=F32)
    na = a_ref.shape[1]
    a_ref[...] = p[:, :na]
    c_q = p[:, na:na + B_Q_RANK]
    c_kv = p[:, na + B_Q_RANK:na + B_Q_RANK + B_KV_RANK]
    k0 = na + B_Q_RANK + B_KV_RANK
    kr_ref[...] = (p[:, k0:k0 + LANES] * ck_ref[...] + p[:, k0 + LANES:k0 + 2 * LANES] * sk_ref[...]).astype(BF16)
    q = jnp.dot(_rms(c_q, qg_ref[...]).astype(BF16), wuq_ref[...], preferred_element_type=F32)
    nn = qn_ref.shape[1]
    nr = qr_ref.shape[1]
    qn_ref[...] = (q[:, :nn] * scale).astype(BF16)
    qr_ref[...] = ((q[:, nn:nn + nr] * cq_ref[...] + q[:, nn + nr:nn + 2 * nr] * sq_ref[...]) * scale).astype(BF16)
    kv = jnp.dot(_rms(c_kv, kvg_ref[...]).astype(BF16), wukv_ref[...], preferred_element_type=F32)
    kn_ref[...] = kv[:, :nn].astype(BF16)
    v_ref[...] = kv[:, nn:].astype(BF16)


def _rot_half_cols(w, width):
    k, n = w.shape
    w3 = w.reshape(k, n // width, 2, width // 2)
    return jnp.stack([-w3[:, :, 1], w3[:, :, 0]], axis=2).reshape(k, n)


def _ab_in_proj(x, seq, g, w_in, q_g, w_uq, kv_g, w_ukv, *, tm=512):
    n, d = x.shape
    a_cols = 3 * A_HEADS * HEAD_DIM
    a_scale = HEAD_DIM ** -0.5
    w_a = jnp.concatenate([w_in[:, :A_HEADS * HEAD_DIM] * a_scale, w_in[:, A_HEADS * HEAD_DIM:a_cols]], axis=1)
    w_lat = w_in[:, a_cols:a_cols + B_Q_RANK + B_KV_RANK]
    w_kr = w_in[:, a_cols + B_Q_RANK + B_KV_RANK:]
    reps = LANES // B_ROPE
    w_full = jnp.concatenate(
        [w_a, w_lat, jnp.tile(w_kr, (1, reps)), jnp.tile(_rot_half_cols(w_kr, B_ROPE), (1, reps))], axis=1).astype(BF16)
    uq = w_uq.reshape(B_Q_RANK, B_HEADS, B_NOPE + B_ROPE)
    uq_n = uq[:, :, :B_NOPE].reshape(B_Q_RANK, B_HEADS * B_NOPE)
    uq_r = uq[:, :, B_NOPE:].reshape(B_Q_RANK, B_HEADS * B_ROPE)
    w_uq_full = jnp.concatenate([uq_n, uq_r, _rot_half_cols(uq_r, B_ROPE)], axis=1).astype(BF16)
    ukv = w_ukv.reshape(B_KV_RANK, B_HEADS, 2, B_NOPE)
    w_ukv_full = jnp.concatenate([ukv[:, :, 0].reshape(B_KV_RANK, -1), ukv[:, :, 1].reshape(B_KV_RANK, -1)], axis=1).astype(BF16)
    inv = ROPE_THETA ** (-jnp.arange(0, B_ROPE, 2, dtype=F32) / B_ROPE)
    ang = jnp.arange(seq, dtype=F32)[:, None] * inv[None, :]
    cos2 = jnp.concatenate([jnp.cos(ang)] * 2, axis=1)
    sin2 = jnp.concatenate([jnp.sin(ang)] * 2, axis=1)
    cq, sq = jnp.tile(cos2, (1, B_HEADS)), jnp.tile(sin2, (1, B_HEADS))
    ck, sk = jnp.tile(cos2, (1, reps)), jnp.tile(sin2, (1, reps))

    row = lambda i: (i, 0)
    per_seq = seq // tm
    pos = lambda i: (i % per_seq, 0)
    nq_n = B_HEADS * B_NOPE
    nq_r = B_HEADS * B_ROPE
    outs = [(a_cols, F32), (nq_n, BF16), (nq_r, BF16), (nq_n, BF16), (LANES, BF16), (nq_n, BF16)]
    return pl.pallas_call(
        functools.partial(_ab_in_kernel, scale=(B_NOPE + B_ROPE) ** -0.5),
        grid=(n // tm,),
        in_specs=[pl.BlockSpec((tm, d), row), _const_spec((1, d)), _const_spec(w_full.shape),
                  _const_spec((1, B_Q_RANK)), _const_spec(w_uq_full.shape),
                  _const_spec((1, B_KV_RANK)), _const_spec(w_ukv_full.shape),
                  pl.BlockSpec((tm, nq_r), pos), pl.BlockSpec((tm, nq_r), pos),
                  pl.BlockSpec((tm, LANES), pos), pl.BlockSpec((tm, LANES), pos)],
        out_specs=[pl.BlockSpec((tm, c), row) for c, _ in outs],
        out_shape=[jax.ShapeDtypeStruct((n, c), dt) for c, dt in outs],
        compiler_params=_params(("parallel",)),
        name="ab_in_proj",
    )(x, g.reshape(1, d), w_full, q_g.reshape(1, -1), w_uq_full, kv_g.reshape(1, -1), w_ukv_full, cq, sq, ck, sk)


def _low_head_lanes():
    return lax.broadcasted_iota(jnp.int32, (1, LANES), 1) < HEAD_DIM


def _stack_heads(x, lo):
    zero = jnp.zeros_like(x)
    return jnp.concatenate([jnp.where(lo, x, zero), jnp.where(lo, zero, x)], axis=0)


def _pv(p, v, lo):
    r = p.shape[0] // 2
    zero = jnp.zeros_like(v)
    return (jnp.dot(p[:r], jnp.where(lo, v, zero), preferred_element_type=F32)
            + jnp.dot(p[r:], jnp.where(lo, zero, v), preferred_element_type=F32))


def _mask_last_block(x, keep, fill):
    rc = keep.shape[0]
    last = jnp.where(keep, x[:, -rc:], jnp.full((rc, rc), fill, x.dtype))
    return last if x.shape[1] == rc else jnp.concatenate([x[:, :-rc], last], axis=1)


def _unstack(x, lo):
    r = x.shape[0] // 2
    return jnp.where(lo, x[:r], x[r:])


def _dilated_kernel(*refs, n_lb, unroll):
    q_refs, k_refs, v_refs = refs[:n_lb], refs[n_lb:2 * n_lb], refs[2 * n_lb:3 * n_lb]
    bias_ref, o_ref, qd, kd, vd, lwd, od, lws, os_ = refs[3 * n_lb:]
    seq = o_ref.shape[0]
    lo = _low_head_lanes()

    def run_branch(bi, nb, src_q, src_k, src_v, dst_lw, dst_o):
        def blocks(it, _):
            work = []
            for u in range(unroll):
                idx = it * unroll + u
                base = pl.multiple_of(idx * BLOCK, BLOCK)
                cur = pl.ds(base, BLOCK)
                prv = pl.ds(pl.multiple_of(jnp.maximum(base - BLOCK, 0), BLOCK), BLOCK)
                pen = jnp.where((idx % nb) == 0, NEG_BIG, 0.0).astype(F32)
                work += [(lb, cur, prv, pen) for lb in range(n_lb)]
            scores = []
            for lb, cur, prv, pen in work:
                qs = _stack_heads(src_q[lb][cur, :].astype(BF16), lo)
                s_c = (lax.dot_general(qs, src_k[lb][cur, :].astype(BF16), _NT, preferred_element_type=F32)
                       + bias_ref[bi, lb, :, BLOCK:])
                s_p = None
                if nb > 1:
                    s_p = (lax.dot_general(qs, src_k[lb][prv, :].astype(BF16), _NT, preferred_element_type=F32)
                           + bias_ref[bi, lb, :, :BLOCK] + pen)
                scores.append((s_c, s_p))
            probs = []
            for s_c, s_p in scores:
                m = jnp.max(s_c if s_p is None else jnp.maximum(s_c, s_p), axis=-1, keepdims=True)
                p_c = jnp.exp(s_c - m)
                p_p = None if s_p is None else jnp.exp(s_p - m)
                l = jnp.sum(p_c if s_p is None else p_c + p_p, axis=-1, keepdims=True)
                probs.append((m, l, p_c, p_p))
            for (lb, cur, prv, _), (m, l, p_c, p_p) in zip(work, probs):
                acc = _pv(p_c.astype(BF16), src_v[lb][cur, :].astype(BF16), lo)
                if nb > 1:
                    acc = acc + _pv(p_p.astype(BF16), src_v[lb][prv, :].astype(BF16), lo)
                dst_o[lb][cur, :] = acc * _unstack(1.0 / l, lo)
                dst_lw[lb][cur, :] = _unstack(m + jnp.log(l), lo)
            return 0

        lax.fori_loop(0, seq // BLOCK // unroll, blocks, 0)

    per_lb = lambda ref, *lead: [ref.at[(*lead, lb)] for lb in range(n_lb)]
    for bi, (_, dil) in enumerate(A_BRANCHES):
        sub = seq // dil
        nb = -(-sub // BLOCK)
        if dil == 1:
            run_branch(bi, nb, q_refs, k_refs, v_refs, per_lb(lws, bi), per_lb(os_, bi))
            continue
        for lb in range(n_lb):
            for r in range(dil):
                rows = pl.ds(r, sub, stride=dil)
                qd[lb, r * sub:(r + 1) * sub, :] = q_refs[lb][rows, :].astype(BF16)
                kd[lb, r * sub:(r + 1) * sub, :] = k_refs[lb][rows, :].astype(BF16)
                vd[lb, r * sub:(r + 1) * sub, :] = v_refs[lb][rows, :].astype(BF16)
        run_branch(bi, nb, per_lb(qd), per_lb(kd), per_lb(vd), per_lb(lwd), per_lb(od))
        for lb in range(n_lb):
            for r in range(dil):
                rows = pl.ds(r, sub, stride=dil)
                lws[bi, lb, rows, :] = lwd[lb, r * sub:(r + 1) * sub, :]
                os_[bi, lb, rows, :] = od[lb, r * sub:(r + 1) * sub, :]

    nbr = len(A_BRANCHES)

    def merge(idx, _):
        rows = pl.ds(pl.multiple_of(idx * BLOCK, BLOCK), BLOCK)
        for lb in range(n_lb):
            lw = [lws[i, lb, rows, :] for i in range(nbr)]
            top = functools.reduce(jnp.maximum, lw)
            w = [jnp.exp(x - top) for x in lw]
            num = sum(w[i] * os_[i, lb, rows, :] for i in range(nbr))
            o_ref[rows, lb * LANES:(lb + 1) * LANES] = (num / sum(w)).astype(BF16)
        return 0

    lax.fori_loop(0, seq // BLOCK, merge, 0)


def _alibi_bias():
    slopes = 2.0 ** (-8.0 * np.arange(1, A_HEADS + 1, dtype=np.float64) / A_HEADS)
    qi = np.arange(BLOCK)[:, None]
    kk = np.arange(2 * BLOCK)[None, :]
    rel = qi + BLOCK - kk
    out = np.empty((len(A_BRANCHES), A_HEADS, BLOCK, 2 * BLOCK), np.float32)
    for bi, (window, dil) in enumerate(A_BRANCHES):
        valid = (rel >= 0) & (rel <= window // dil)
        bias = -slopes[:, None, None] * (rel * dil)[None]
        out[bi] = np.where(valid[None], bias, NEG_BIG)
    return jnp.asarray(out.reshape(len(A_BRANCHES), A_HEADS // 2, 2 * BLOCK, 2 * BLOCK))


def _dilated_attn(a_qkv, batch, seq, *, n_lb=2, unroll=4):
    a3 = a_qkv.reshape(batch, seq, a_qkv.shape[1])
    width = n_lb * LANES
    groups = A_HEADS * HEAD_DIM // width
    bias = _alibi_bias()
    blk = lambda t, lb: pl.BlockSpec((None, seq, LANES), lambda g, b: (b, 0, (t * groups + g) * n_lb + lb))
    nbr = len(A_BRANCHES)
    out = pl.pallas_call(
        functools.partial(_dilated_kernel, n_lb=n_lb, unroll=unroll),
        grid=(groups, batch),
        in_specs=[blk(t, lb) for t in range(3) for lb in range(n_lb)]
                 + [pl.BlockSpec((nbr, n_lb, 2 * BLOCK, 2 * BLOCK), lambda g, b: (0, g, 0, 0))],
        out_specs=pl.BlockSpec((None, seq, width), lambda g, b: (b, 0, g)),
        out_shape=jax.ShapeDtypeStruct((batch, seq, groups * width), BF16),
        scratch_shapes=[pltpu.VMEM((n_lb, seq, LANES), BF16)] * 3 + [pltpu.VMEM((n_lb, seq, LANES), F32)] * 2
                       + [pltpu.VMEM((nbr, n_lb, seq, LANES), F32)] * 2,
        compiler_params=_params(("parallel", "parallel")),
        name="dilated_attn",
    )(*([a3] * (3 * n_lb)), bias)
    return out.reshape(batch * seq, groups * width)


def _mla_kernel(qn_ref, qr_ref, kn_ref, kr_ref, v_ref, o_ref, qs_ref, vs_ref, *, tq, tk, rc, skew):
    i = pl.program_id(2)
    n_lb = qn_ref.shape[1] // LANES
    n_rc = tq // rc
    lane = lax.broadcasted_iota(jnp.int32, (1, LANES), 1)
    lo = lane < HEAD_DIM
    causal = lax.broadcasted_iota(jnp.int32, (rc, rc), 1) <= lax.broadcasted_iota(jnp.int32, (rc, rc), 0)

    qr = qr_ref[...]
    for lb in range(n_lb):
        qn = qn_ref[:, lb * LANES:(lb + 1) * LANES]
        for e in range(2):
            slot = 2 * lb + e
            in_slot = (lane >= slot * B_ROPE) & (lane < (slot + 1) * B_ROPE)
            qs_ref[lb, e, :, :LANES] = jnp.where(lo if e == 0 else ~lo, qn, jnp.zeros_like(qn))
            qs_ref[lb, e, :, LANES:] = jnp.where(in_slot, qr, jnp.zeros_like(qr))

    @pl.when(i == 0)
    def _():
        for lb in range(n_lb):
            v = v_ref[:, lb * LANES:(lb + 1) * LANES]
            one = jnp.ones_like(v)
            vs_ref[lb, 0] = jnp.where(lo, v, one)
            vs_ref[lb, 1] = jnp.where(lo, one, v)

    chains = [(lb, c, e) for lb in range(n_lb) for c in range(n_rc) for e in range(2)]

    def step(kb, carry, diagonal):
        base = pl.multiple_of(kb * (tq if diagonal else tk), tk)
        width = lambda c: (c + 1) * rc if diagonal else tk
        n_ch = len(chains)
        ss, new_m, alphas, ps, new_acc = ([None] * n_ch for _ in range(5))

        def scores(n):
            lb, c, e = chains[n]
            rows = pl.ds(base, width(c))
            k = jnp.concatenate([kn_ref[rows, lb * LANES:(lb + 1) * LANES], kr_ref[rows, :]], axis=1)
            ss[n] = lax.dot_general(qs_ref[lb, e, c * rc:(c + 1) * rc, :], k, _NT, preferred_element_type=F32)

        def probabilities(n):
            lb, c, e = chains[n]
            s = ss[n]
            if diagonal:
                s = _mask_last_block(s, causal, NEG_BIG)
            new_m[n] = jnp.maximum(carry[0][n], jnp.max(s, axis=-1, keepdims=True))
            alphas[n] = jnp.exp(carry[0][n] - new_m[n])
            ps[n] = jnp.exp(s - new_m[n]).astype(BF16)

        def outputs(n):
            lb, c, e = chains[n]
            ve = vs_ref[lb, e, pl.ds(base, width(c)), :]
            new_acc[n] = alphas[n] * carry[1][n] + jnp.dot(ps[n], ve, preferred_element_type=F32)

        for n in range(n_ch + 2 * skew):
            for stage, m in ((scores, n), (probabilities, n - skew), (outputs, n - 2 * skew)):
                if 0 <= m < n_ch:
                    stage(m)
        return tuple(new_m), tuple(new_acc)

    init = (tuple(jnp.full((rc, 1), NEG_BIG, F32) for _ in chains),
            tuple(jnp.zeros((rc, LANES), F32) for _ in chains))
    carry = lax.fori_loop(0, i * (tq // tk), functools.partial(step, diagonal=False), init)
    _, acc = step(i, carry, True)
    for lb in range(n_lb):
        for c in range(n_rc):
            n = (lb * n_rc + c) * 2
            swap = lambda x: jnp.concatenate([x[:, HEAD_DIM:], x[:, :HEAD_DIM]], axis=1)
            out = [acc[n + e] / swap(acc[n + e]) for e in range(2)]
            o_ref[c * rc:(c + 1) * rc, lb * LANES:(lb + 1) * LANES] = jnp.where(lo, out[0], out[1]).astype(BF16)


def _mla_attn(qn, qr, kn, kr, v, batch, seq, *, tq=512, tk=256, rc=128, skew=8):
    width = (LANES // B_ROPE) * B_NOPE
    groups = qn.shape[1] // width
    r3 = lambda t: t.reshape(batch, seq, t.shape[1])
    out = pl.pallas_call(
        functools.partial(_mla_kernel, tq=tq, tk=tk, rc=rc, skew=skew),
        grid=(batch, groups, seq // tq),
        in_specs=[pl.BlockSpec((None, tq, width), lambda b, g, i: (b, i, g)),
                  pl.BlockSpec((None, tq, LANES), lambda b, g, i: (b, i, g)),
                  pl.BlockSpec((None, seq, width), lambda b, g, i: (b, 0, g)),
                  pl.BlockSpec((None, seq, LANES), lambda b, g, i: (b, 0, 0)),
                  pl.BlockSpec((None, seq, width), lambda b, g, i: (b, 0, g))],
        out_specs=pl.BlockSpec((None, tq, width), lambda b, g, i: (b, i, g)),
        out_shape=jax.ShapeDtypeStruct((batch, seq, groups * width), BF16),
        scratch_shapes=[pltpu.VMEM((width // LANES, 2, tq, 2 * LANES), BF16),
                        pltpu.VMEM((width // LANES, 2, seq, LANES), BF16)],
        compiler_params=_params(("parallel", "parallel", "arbitrary")),
        name="mla_attn",
    )(r3(qn), r3(qr), r3(kn), r3(kr), r3(v))
    return out.reshape(batch * seq, groups * width)


def _norm_proj_kernel(x_ref, g_ref, w_ref, o_ref):
    h = _rms(x_ref[...], g_ref[...]).astype(BF16)
    o_ref[...] = jnp.dot(h, w_ref[...], preferred_element_type=F32).astype(o_ref.dtype)


def _sb_in_proj(x, g, w_in, *, tm=512):
    n, d = x.shape
    nq = C_HEADS * HEAD_DIM
    w = jnp.concatenate([w_in[:, :nq] * HEAD_DIM ** -0.5, w_in[:, nq:]], axis=1).astype(BF16)
    return pl.pallas_call(
        _norm_proj_kernel,
        grid=(n // tm,),
        in_specs=[pl.BlockSpec((tm, d), lambda i: (i, 0)), _const_spec((1, d)), _const_spec(w.shape)],
        out_specs=pl.BlockSpec((tm, w.shape[1]), lambda i: (i, 0)),
        out_shape=jax.ShapeDtypeStruct((n, w.shape[1]), BF16),
        compiler_params=_params(("parallel",)),
        name="sb_in_proj",
    )(x, g.reshape(1, d), w)


def _sb_kernel(q_ref, k_ref, v_ref, tri_ref, o_ref, qs_ref, vs_ref, *, tq, tk, rc, skew):
    i = pl.program_id(2)
    n_lb = q_ref.shape[1] // LANES
    n_rc = tq // rc
    lo = _low_head_lanes()
    strict = lax.broadcasted_iota(jnp.int32, (rc, rc), 1) < lax.broadcasted_iota(jnp.int32, (rc, rc), 0)
    for lb in range(n_lb):
        x = q_ref[:, lb * LANES:(lb + 1) * LANES]
        zero = jnp.zeros_like(x)
        qs_ref[lb, 0] = jnp.where(lo, x, zero)
        qs_ref[lb, 1] = jnp.where(lo, zero, x)

    @pl.when(i == 0)
    def _():
        for lb in range(n_lb):
            v = v_ref[:, lb * LANES:(lb + 1) * LANES]
            zero = jnp.zeros_like(v)
            vs_ref[lb, 0] = jnp.where(lo, v, zero)
            vs_ref[lb, 1] = jnp.where(lo, zero, v)

    chains = [(lb, c, e) for lb in range(n_lb) for c in range(n_rc) for e in range(2)]

    def step(kb, carry, diagonal):
        base = pl.multiple_of(kb * (tq if diagonal else tk), tk)
        width = lambda c: (c + 1) * rc if diagonal else tk
        n_ch = len(chains)
        zs, incls, new_done, des = [None] * n_ch, [None] * n_ch, [None] * n_ch, [None] * n_ch

        def scores(n):
            lb, c, e = chains[n]
            zs[n] = lax.dot_general(qs_ref[lb, e, c * rc:(c + 1) * rc, :],
                                    k_ref[pl.ds(base, width(c)), lb * LANES:(lb + 1) * LANES], _NT,
                                    preferred_element_type=F32)

        def cumulative(n):
            lb, c, e = chains[n]
            zb = zs[n].astype(BF16)
            sp = jnp.maximum(zb, 0) + jnp.log(1 + jnp.exp(-jnp.abs(zb)))
            if diagonal:
                sp = _mask_last_block(sp, strict, 0)
            tri = tri_ref[:width(c), :width(c)]
            incls[n] = jnp.dot(sp, tri, preferred_element_type=F32)

        def outputs(n):
            lb, c, e = chains[n]
            a = jnp.exp(zs[n] - incls[n])
            if diagonal:
                a = _mask_last_block(a, strict, 0)
            de = jnp.dot(a.astype(BF16), vs_ref[lb, e, pl.ds(base, width(c)), :], preferred_element_type=F32)
            total = jnp.broadcast_to(incls[n][:, 0:1], (rc, LANES))
            if diagonal:
                new_done[n] = total
            else:
                de = de * jnp.exp(-carry[0][n])
                new_done[n] = carry[0][n] + total
            des[n] = de

        for n in range(n_ch + 2 * skew):
            for stage, m in ((scores, n), (cumulative, n - skew), (outputs, n - 2 * skew)):
                if 0 <= m < n_ch:
                    stage(m)
        new_acc = [des[2 * n] + des[2 * n + 1] for n in range(len(chains) // 2)]
        if not diagonal:
            new_acc = [carry[1][n] + new_acc[n] for n in range(len(new_acc))]
        return tuple(new_done), tuple(new_acc)

    carry = step(i, None, True)
    n_below = i * (tq // tk)
    _, acc = lax.fori_loop(0, n_below, lambda t, c: step(n_below - 1 - t, c, False), carry)
    for lb in range(n_lb):
        for c in range(n_rc):
            o_ref[c * rc:(c + 1) * rc, lb * LANES:(lb + 1) * LANES] = acc[lb * n_rc + c].astype(BF16)


def _sb_attn(qkv, batch, seq, *, tq=512, tk=256, n_lb=2, rc=128, skew=8):
    width = n_lb * LANES
    groups = C_HEADS * HEAD_DIM // width
    q3 = qkv.reshape(batch, seq, qkv.shape[1])
    tri = jnp.asarray(np.tril(np.ones((tq, tq), np.float32)), BF16)
    out = pl.pallas_call(
        functools.partial(_sb_kernel, tq=tq, tk=tk, rc=rc, skew=skew),
        grid=(batch, groups, seq // tq),
        in_specs=[pl.BlockSpec((None, tq, width), lambda b, g, i: (b, i, g)),
                  pl.BlockSpec((None, seq, width), lambda b, g, i: (b, 0, groups + g)),
                  pl.BlockSpec((None, seq, width), lambda b, g, i: (b, 0, 2 * groups + g)),
                  pl.BlockSpec((tq, tq), lambda b, g, i: (0, 0))],
        out_specs=pl.BlockSpec((None, tq, width), lambda b, g, i: (b, i, g)),
        out_shape=jax.ShapeDtypeStruct((batch, seq, groups * width), BF16),
        scratch_shapes=[pltpu.VMEM((n_lb, 2, tq, LANES), BF16), pltpu.VMEM((n_lb, 2, seq, LANES), BF16)],
        compiler_params=_params(("parallel", "parallel", "arbitrary")),
        name="sb_attn",
    )(q3, q3, q3, tri)
    return out.reshape(batch * seq, groups * width)


def kernel(x, ffn_norm_g, mix_norm_g, ffn_w_gate, ffn_w_up, ffn_w_down, ab_w_in, mla_q_norm_g,
           mla_w_uq, mla_kv_norm_g, mla_w_ukv, ab_w_out, sb_w_in, sb_w_out, final_norm_g):
    batch, seq, d = x.shape
    depth = ffn_norm_g.shape[0]
    h = x.reshape(batch * seq, d)
    bf = lambda w: w.astype(BF16)

    wg, wu, wd = bf(ffn_w_gate), bf(ffn_w_up), bf(ffn_w_down)

    def ffn(h, pre, i, s, final_g=None):
        return _ffn(h, pre, ffn_norm_g[i, s], wg, wu, wd, (i, s), final_g)

    for i in range(depth):
        h = ffn(h, [], i, 0)
        if i % 2 == 0:
            e = i // 2
            a_qkv, qn, qr, kn, kr, v = _ab_in_proj(h, seq, mix_norm_g[i], ab_w_in[e], mla_q_norm_g[e],
                                                   mla_w_uq[e], mla_kv_norm_g[e], mla_w_ukv[e])
            o_a = _dilated_attn(a_qkv, batch, seq)
            o_b = _mla_attn(qn, qr, kn, kr, v, batch, seq)
            w_out = bf(ab_w_out[e])
            na = o_a.shape[1]
            pre = [(o_a, w_out[:na]), (o_b, w_out[na:])]
        else:
            o = i // 2
            qkv = _sb_in_proj(h, mix_norm_g[i], sb_w_in[o])
            pre = [(_sb_attn(qkv, batch, seq), bf(sb_w_out[o]))]
        h = ffn(h, pre, i, 1, final_norm_g if i == depth - 1 else None)
    return h.reshape(batch, seq, d)
```

```python
import functools

import numpy as np
import jax
import jax.numpy as jnp
from jax import lax
from jax.experimental import pallas as pl
from jax.experimental.pallas import tpu as pltpu

F32 = jnp.float32
BF16 = jnp.bfloat16

NORM_EPS = 1e-6
ROPE_THETA = 10000.0
LANES = 128
HEAD_DIM = 64
BLOCK = 128
A_HEADS = 8
A_BRANCHES = ((128, 1), (512, 4), (2048, 16))
B_HEADS = 8
B_NOPE = 64
B_ROPE = 32
B_Q_RANK = 256
B_KV_RANK = 128
C_HEADS = 16
NEG_BIG = -1e30
LOG2E = 1.4426950408889634
VMEM_LIMIT = 58 * 1024 * 1024

_NT = (((1,), (1,)), ((), ()))


def _rms(x, g):
    return x * lax.rsqrt(jnp.mean(x * x, axis=-1, keepdims=True) + NORM_EPS) * g


def _const_spec(shape):
    nd = len(shape)
    return pl.BlockSpec(shape, lambda *_: (0,) * nd, pipeline_mode=pl.Buffered(1))


def _params(sem):
    return pltpu.CompilerParams(dimension_semantics=sem, vmem_limit_bytes=VMEM_LIMIT)


def _ffn_kernel(*refs, n_pre, final_norm, tf):
    x_ref = refs[0]
    pre = refs[1:1 + 2 * n_pre]
    g_ref, wg_ref, wu_ref, wd_ref = refs[1 + 2 * n_pre:5 + 2 * n_pre]
    rest = refs[5 + 2 * n_pre:]
    fg_ref = rest[0] if final_norm else None
    o_ref = rest[-1]

    x = x_ref[...]
    for p in range(n_pre):
        x = x + jnp.dot(pre[2 * p][...], pre[2 * p + 1][...], preferred_element_type=F32)
    h = _rms(x, g_ref[...]).astype(BF16)
    d_ff = wg_ref.shape[1]
    acc = None
    for c in range(d_ff // tf):
        sl = slice(c * tf, (c + 1) * tf)
        gate = jnp.dot(h, wg_ref[:, sl], preferred_element_type=F32)
        up = jnp.dot(h, wu_ref[:, sl], preferred_element_type=F32)
        a = (gate * jax.nn.sigmoid(gate) * up).astype(BF16)
        d = jnp.dot(a, wd_ref[sl, :], preferred_element_type=F32)
        acc = d if acc is None else acc + d
    y = x + 0.5 * acc
    if final_norm:
        y = _rms(y, fg_ref[...])
    o_ref[...] = y


def _ffn(x, pre, g, wg, wu, wd, which, final_g=None, *, tm=512, tf=256):
    n, d = x.shape
    row = lambda i: (i, 0)
    picked = lambda w: pl.BlockSpec((None, None) + w.shape[2:], lambda i: which + (0, 0),
                                    pipeline_mode=pl.Buffered(1))
    args = [x]
    specs = [pl.BlockSpec((tm, d), row)]
    for o, w in pre:
        args += [o, w]
        specs += [pl.BlockSpec((tm, o.shape[1]), row), _const_spec(w.shape)]
    args += [g.reshape(1, d), wg, wu, wd]
    specs += [_const_spec((1, d)), picked(wg), picked(wu), picked(wd)]
    if final_g is not None:
        args.append(final_g.reshape(1, d))
        specs.append(_const_spec((1, d)))
    return pl.pallas_call(
        functools.partial(_ffn_kernel, n_pre=len(pre), final_norm=final_g is not None, tf=tf),
        grid=(n // tm,),
        in_specs=specs,
        out_specs=pl.BlockSpec((tm, d), row),
        out_shape=jax.ShapeDtypeStruct((n, d), F32),
        compiler_params=_params(("parallel",)),
        name="ffn",
    )(*args)


def _ab_in_kernel(x_ref, g_ref, win_ref, qg_ref, wuq_ref, kvg_ref, wukv_ref,
                  cq_ref, sq_ref, ck_ref, sk_ref,
                  a_ref, qn_ref, qr_ref, kn_ref, kr_ref, v_ref, *, scale):
    h = _rms(x_ref[...], g_ref[...]).astype(BF16)
    p = jnp.dot(h, win_ref[...], preferred_element_type=F32)
    na = a_ref.shape[1]
    a_ref[...] = p[:, :na]
    c_q = p[:, na:na + B_Q_RANK]
    c_kv = p[:, na + B_Q_RANK:na + B_Q_RANK + B_KV_RANK]
    k0 = na + B_Q_RANK + B_KV_RANK
    kr_ref[...] = (p[:, k0:k0 + LANES] * ck_ref[...] + p[:, k0 + LANES:k0 + 2 * LANES] * sk_ref[...]).astype(BF16)
    q = jnp.dot(_rms(c_q, qg_ref[...]).astype(BF16), wuq_ref[...], preferred_element_type=F32)
    nn = qn_ref.shape[1]
    nr = qr_ref.shape[1]
    qn_ref[...] = (q[:, :nn] * scale).astype(BF16)
    qr_ref[...] = ((q[:, nn:nn + nr] * cq_ref[...] + q[:, nn + nr:nn + 2 * nr] * sq_ref[...]) * scale).astype(BF16)
    kv = jnp.dot(_rms(c_kv, kvg_ref[...]).astype(BF16), wukv_ref[...], preferred_element_type=F32)
    kn_ref[...] = kv[:, :nn].astype(BF16)
    v_ref[...] = kv[:, nn:].astype(BF16)


def _rot_half_cols(w, width):
    k, n = w.shape
    w3 = w.reshape(k, n // width, 2, width // 2)
    return jnp.stack([-w3[:, :, 1], w3[:, :, 0]], axis=2).reshape(k, n)


def _ab_in_proj(x, seq, g, w_in, q_g, w_uq, kv_g, w_ukv, *, tm=512):
    n, d = x.shape
    a_cols = 3 * A_HEADS * HEAD_DIM
    a_scale = HEAD_DIM ** -0.5
    w_a = jnp.concatenate([w_in[:, :A_HEADS * HEAD_DIM] * a_scale, w_in[:, A_HEADS * HEAD_DIM:a_cols]], axis=1)
    w_lat = w_in[:, a_cols:a_cols + B_Q_RANK + B_KV_RANK]
    w_kr = w_in[:, a_cols + B_Q_RANK + B_KV_RANK:]
    reps = LANES // B_ROPE
    w_full = jnp.concatenate(
        [w_a, w_lat, jnp.tile(w_kr, (1, reps)), jnp.tile(_rot_half_cols(w_kr, B_ROPE), (1, reps))], axis=1).astype(BF16)
    uq = w_uq.reshape(B_Q_RANK, B_HEADS, B_NOPE + B_ROPE)
    uq_n = uq[:, :, :B_NOPE].reshape(B_Q_RANK, B_HEADS * B_NOPE)
    uq_r = uq[:, :, B_NOPE:].reshape(B_Q_RANK, B_HEADS * B_ROPE)
    w_uq_full = jnp.concatenate([uq_n, uq_r, _rot_half_cols(uq_r, B_ROPE)], axis=1).astype(BF16)
    ukv = w_ukv.reshape(B_KV_RANK, B_HEADS, 2, B_NOPE)
    w_ukv_full = jnp.concatenate([ukv[:, :, 0].reshape(B_KV_RANK, -1), ukv[:, :, 1].reshape(B_KV_RANK, -1)], axis=1).astype(BF16)
    inv = ROPE_THETA ** (-jnp.arange(0, B_ROPE, 2, dtype=F32) / B_ROPE)
    ang = jnp.arange(seq, dtype=F32)[:, None] * inv[None, :]
    cos2 = jnp.concatenate([jnp.cos(ang)] * 2, axis=1)
    sin2 = jnp.concatenate([jnp.sin(ang)] * 2, axis=1)
    cq, sq = jnp.tile(cos2, (1, B_HEADS)), jnp.tile(sin2, (1, B_HEADS))
    ck, sk = jnp.tile(cos2, (1, reps)), jnp.tile(sin2, (1, reps))

    row = lambda i: (i, 0)
    per_seq = seq // tm
    pos = lambda i: (i % per_seq, 0)
    nq_n = B_HEADS * B_NOPE
    nq_r = B_HEADS * B_ROPE
    outs = [(a_cols, F32), (nq_n, BF16), (nq_r, BF16), (nq_n, BF16), (LANES, BF16), (nq_n, BF16)]
    return pl.pallas_call(
        functools.partial(_ab_in_kernel, scale=(B_NOPE + B_ROPE) ** -0.5),
        grid=(n // tm,),
        in_specs=[pl.BlockSpec((tm, d), row), _const_spec((1, d)), _const_spec(w_full.shape),
                  _const_spec((1, B_Q_RANK)), _const_spec(w_uq_full.shape),
                  _const_spec((1, B_KV_RANK)), _const_spec(w_ukv_full.shape),
                  pl.BlockSpec((tm, nq_r), pos), pl.BlockSpec((tm, nq_r), pos),
                  pl.BlockSpec((tm, LANES), pos), pl.BlockSpec((tm, LANES), pos)],
        out_specs=[pl.BlockSpec((tm, c), row) for c, _ in outs],
        out_shape=[jax.ShapeDtypeStruct((n, c), dt) for c, dt in outs],
        compiler_params=_params(("parallel",)),
        name="ab_in_proj",
    )(x, g.reshape(1, d), w_full, q_g.reshape(1, -1), w_uq_full, kv_g.reshape(1, -1), w_ukv_full, cq, sq, ck, sk)


def _low_head_lanes():
    return lax.broadcasted_iota(jnp.int32, (1, LANES), 1) < HEAD_DIM


def _stack_heads(x, lo):
    zero = jnp.zeros_like(x)
    return jnp.concatenate([jnp.where(lo, x, zero), jnp.where(lo, zero, x)], axis=0)


def _pv(p, v, lo):
    r = p.shape[0] // 2
    zero = jnp.zeros_like(v)
    return (jnp.dot(p[:r], jnp.where(lo, v, zero), preferred_element_type=F32)
            + jnp.dot(p[r:], jnp.where(lo, zero, v), preferred_element_type=F32))


def _unstack(x, lo):
    r = x.shape[0] // 2
    return jnp.where(lo, x[:r], x[r:])


def _dilated_kernel(*refs, n_lb, unroll):
    q_refs, k_refs, v_refs = refs[:n_lb], refs[n_lb:2 * n_lb], refs[2 * n_lb:3 * n_lb]
    bias_ref, o_ref, qd, kd, vd, lwd, od, lws, os_ = refs[3 * n_lb:]
    seq = o_ref.shape[0]
    lo = _low_head_lanes()

    def run_branch(bi, nb, src_q, src_k, src_v, dst_lw, dst_o):
        def blocks(it, _):
            work = []
            for u in range(unroll):
                idx = it * unroll + u
                base = pl.multiple_of(idx * BLOCK, BLOCK)
                cur = pl.ds(base, BLOCK)
                prv = pl.ds(pl.multiple_of(jnp.maximum(base - BLOCK, 0), BLOCK), BLOCK)
                pen = jnp.where((idx % nb) == 0, NEG_BIG, 0.0).astype(F32)
                work += [(lb, cur, prv, pen) for lb in range(n_lb)]
            scores = []
            for lb, cur, prv, pen in work:
                qs = _stack_heads(src_q[lb][cur, :].astype(BF16), lo)
                s_c = (lax.dot_general(qs, src_k[lb][cur, :].astype(BF16), _NT, preferred_element_type=F32)
                       + bias_ref[bi, lb, :, BLOCK:])
                s_p = None
                if nb > 1:
                    s_p = (lax.dot_general(qs, src_k[lb][prv, :].astype(BF16), _NT, preferred_element_type=F32)
                           + bias_ref[bi, lb, :, :BLOCK] + pen)
                scores.append((s_c, s_p))
            probs = []
            for s_c, s_p in scores:
                m = jnp.max(s_c if s_p is None else jnp.maximum(s_c, s_p), axis=-1, keepdims=True)
                p_c = jnp.exp(s_c - m)
                p_p = None if s_p is None else jnp.exp(s_p - m)
                l = jnp.sum(p_c if s_p is None else p_c + p_p, axis=-1, keepdims=True)
                probs.append((m, l, p_c, p_p))
            for (lb, cur, prv, _), (m, l, p_c, p_p) in zip(work, probs):
                acc = _pv(p_c.astype(BF16), src_v[lb][cur, :].astype(BF16), lo)
                if nb > 1:
                    acc = acc + _pv(p_p.astype(BF16), src_v[lb][prv, :].astype(BF16), lo)
                dst_o[lb][cur, :] = acc * _unstack(1.0 / l, lo)
                dst_lw[lb][cur, :] = _unstack(m + jnp.log(l), lo)
            return 0

        lax.fori_loop(0, seq // BLOCK // unroll, blocks, 0)

    per_lb = lambda ref, *lead: [ref.at[(*lead, lb)] for lb in range(n_lb)]
    for bi, (_, dil) in enumerate(A_BRANCHES):
        sub = seq // dil
        nb = -(-sub // BLOCK)
        if dil == 1:
            run_branch(bi, nb, q_refs, k_refs, v_refs, per_lb(lws, bi), per_lb(os_, bi))
            continue
        for lb in range(n_lb):
            for r in range(dil):
                rows = pl.ds(r, sub, stride=dil)
                qd[lb, r * sub:(r + 1) * sub, :] = q_refs[lb][rows, :].astype(BF16)
                kd[lb, r * sub:(r + 1) * sub, :] = k_refs[lb][rows, :].astype(BF16)
                vd[lb, r * sub:(r + 1) * sub, :] = v_refs[lb][rows, :].astype(BF16)
        run_branch(bi, nb, per_lb(qd), per_lb(kd), per_lb(vd), per_lb(lwd), per_lb(od))
        for lb in range(n_lb):
            for r in range(dil):
                rows = pl.ds(r, sub, stride=dil)
                lws[bi, lb, rows, :] = lwd[lb, r * sub:(r + 1) * sub, :]
                os_[bi, lb, rows, :] = od[lb, r * sub:(r + 1) * sub, :]

    nbr = len(A_BRANCHES)

    def merge(idx, _):
        rows = pl.ds(pl.multiple_of(idx * BLOCK, BLOCK), BLOCK)
        for lb in range(n_lb):
            lw = [lws[i, lb, rows, :] for i in range(nbr)]
            top = functools.reduce(jnp.maximum, lw)
            w = [jnp.exp(x - top) for x in lw]
            num = sum(w[i] * os_[i, lb, rows, :] for i in range(nbr))
            o_ref[rows, lb * LANES:(lb + 1) * LANES] = (num / sum(w)).astype(BF16)
        return 0

    lax.fori_loop(0, seq // BLOCK, merge, 0)


def _alibi_bias():
    slopes = 2.0 ** (-8.0 * np.arange(1, A_HEADS + 1, dtype=np.float64) / A_HEADS)
    qi = np.arange(BLOCK)[:, None]
    kk = np.arange(2 * BLOCK)[None, :]
    rel = qi + BLOCK - kk
    out = np.empty((len(A_BRANCHES), A_HEADS, BLOCK, 2 * BLOCK), np.float32)
    for bi, (window, dil) in enumerate(A_BRANCHES):
        valid = (rel >= 0) & (rel <= window // dil)
        bias = -slopes[:, None, None] * (rel * dil)[None]
        out[bi] = np.where(valid[None], bias, NEG_BIG)
    return jnp.asarray(out.reshape(len(A_BRANCHES), A_HEADS // 2, 2 * BLOCK, 2 * BLOCK))


def _dilated_attn(a_qkv, batch, seq, *, n_lb=2, unroll=4):
    a3 = a_qkv.reshape(batch, seq, a_qkv.shape[1])
    width = n_lb * LANES
    groups = A_HEADS * HEAD_DIM // width
    bias = _alibi_bias()
    blk = lambda t, lb: pl.BlockSpec((None, seq, LANES), lambda g, b: (b, 0, (t * groups + g) * n_lb + lb))
    nbr = len(A_BRANCHES)
    out = pl.pallas_call(
        functools.partial(_dilated_kernel, n_lb=n_lb, unroll=unroll),
        grid=(groups, batch),
        in_specs=[blk(t, lb) for t in range(3) for lb in range(n_lb)]
                 + [pl.BlockSpec((nbr, n_lb, 2 * BLOCK, 2 * BLOCK), lambda g, b: (0, g, 0, 0))],
        out_specs=pl.BlockSpec((None, seq, width), lambda g, b: (b, 0, g)),
        out_shape=jax.ShapeDtypeStruct((batch, seq, groups * width), BF16),
        scratch_shapes=[pltpu.VMEM((n_lb, seq, LANES), BF16)] * 3 + [pltpu.VMEM((n_lb, seq, LANES), F32)] * 2
                       + [pltpu.VMEM((nbr, n_lb, seq, LANES), F32)] * 2,
        compiler_params=_params(("parallel", "parallel")),
        name="dilated_attn",
    )(*([a3] * (3 * n_lb)), bias)
    return out.reshape(batch * seq, groups * width)


V_ROWS = 80


def _mla_t_kernel(qn_ref, qr_ref, kn_ref, kr_ref, v_ref, o_ref, qs_ref, vt_ref, *, tq, tk, cw, skew):
    i = pl.program_id(2)
    n_lb = qn_ref.shape[1] // LANES
    n_cq = tq // cw
    lane = lax.broadcasted_iota(jnp.int32, (1, LANES), 1)
    lo = lane < HEAD_DIM
    causal = lax.broadcasted_iota(jnp.int32, (cw, cw), 0) <= lax.broadcasted_iota(jnp.int32, (cw, cw), 1)

    qr = qr_ref[...]
    for lb in range(n_lb):
        qn = qn_ref[:, lb * LANES:(lb + 1) * LANES]
        for e in range(2):
            slot = 2 * lb + e
            in_slot = (lane >= slot * B_ROPE) & (lane < (slot + 1) * B_ROPE)
            qs_ref[lb, e, :, :LANES] = jnp.where(lo if e == 0 else ~lo, qn, jnp.zeros_like(qn))
            qs_ref[lb, e, :, LANES:] = jnp.where(in_slot, qr, jnp.zeros_like(qr))

    @pl.when(i == 0)
    def _():
        for lb in range(n_lb):
            vt = v_ref[:, lb * LANES:(lb + 1) * LANES].astype(F32).T
            for e in range(2):
                vt_ref[lb, e, :HEAD_DIM, :] = vt[e * HEAD_DIM:(e + 1) * HEAD_DIM].astype(BF16)
                vt_ref[lb, e, HEAD_DIM:, :] = jnp.ones((V_ROWS - HEAD_DIM, vt.shape[1]), BF16)

    chains = [(lb, e, c) for lb in range(n_lb) for e in range(2) for c in range(n_cq)]

    def step(kb, carry, diagonal):
        base = pl.multiple_of(kb * (tq if diagonal else tk), tk)
        width = lambda c: (c + 1) * cw if diagonal else tk
        n_ch = len(chains)
        ss, new_m, alphas, ps, new_acc = ([None] * n_ch for _ in range(5))

        def scores(n):
            lb, e, c = chains[n]
            rows = pl.ds(base, width(c))
            k = jnp.concatenate([kn_ref[rows, lb * LANES:(lb + 1) * LANES], kr_ref[rows, :]], axis=1)
            ss[n] = lax.dot_general(k, qs_ref[lb, e, c * cw:(c + 1) * cw, :], _NT, preferred_element_type=F32)

        def probabilities(n):
            lb, e, c = chains[n]
            s = ss[n]
            if diagonal:
                last = jnp.where(causal, s[-cw:], NEG_BIG)
                s = last if c == 0 else jnp.concatenate([s[:-cw], last], axis=0)
            new_m[n] = jnp.maximum(carry[0][n], jnp.max(s, axis=0, keepdims=True))
            alphas[n] = jnp.exp(carry[0][n] - new_m[n])
            ps[n] = jnp.exp(s - new_m[n]).astype(BF16)

        def outputs(n):
            lb, e, c = chains[n]
            vt = vt_ref[lb, e, :, pl.ds(base, width(c))]
            new_acc[n] = alphas[n] * carry[1][n] + jnp.dot(vt, ps[n], preferred_element_type=F32)

        for n in range(n_ch + 2 * skew):
            for stage, m in ((scores, n), (probabilities, n - skew), (outputs, n - 2 * skew)):
                if 0 <= m < n_ch:
                    stage(m)
        return tuple(new_m), tuple(new_acc)

    init = (tuple(jnp.full((1, cw), NEG_BIG, F32) for _ in chains),
            tuple(jnp.zeros((V_ROWS, cw), F32) for _ in chains))
    carry = lax.fori_loop(0, i * (tq // tk), functools.partial(step, diagonal=False), init)
    _, acc = step(i, carry, True)
    for lb in range(n_lb):
        for c in range(n_cq):
            heads = [acc[(lb * 2 + e) * n_cq + c] for e in range(2)]
            out_t = jnp.concatenate([a[:HEAD_DIM] / a[HEAD_DIM:HEAD_DIM + 1] for a in heads], axis=0)
            o_ref[c * cw:(c + 1) * cw, lb * LANES:(lb + 1) * LANES] = out_t.T.astype(BF16)


def _mla_attn(qn, qr, kn, kr, v, batch, seq, *, tq=512, tk=256, skew=8):
    width = (LANES // B_ROPE) * B_NOPE
    groups = qn.shape[1] // width
    r3 = lambda t: t.reshape(batch, seq, t.shape[1])
    out = pl.pallas_call(
        functools.partial(_mla_t_kernel, tq=tq, tk=tk, cw=2 * LANES, skew=skew),
        grid=(batch, groups, seq // tq),
        in_specs=[pl.BlockSpec((None, tq, width), lambda b, g, i: (b, i, g)),
                  pl.BlockSpec((None, tq, LANES), lambda b, g, i: (b, i, g)),
                  pl.BlockSpec((None, seq, width), lambda b, g, i: (b, 0, g)),
                  pl.BlockSpec((None, seq, LANES), lambda b, g, i: (b, 0, 0)),
                  pl.BlockSpec((None, seq, width), lambda b, g, i: (b, 0, g))],
        out_specs=pl.BlockSpec((None, tq, width), lambda b, g, i: (b, i, g)),
        out_shape=jax.ShapeDtypeStruct((batch, seq, groups * width), BF16),
        scratch_shapes=[pltpu.VMEM((width // LANES, 2, tq, 2 * LANES), BF16),
                        pltpu.VMEM((width // LANES, 2, V_ROWS, seq), BF16)],
        compiler_params=_params(("parallel", "parallel", "arbitrary")),
        name="mla_attn",
    )(r3(qn), r3(qr), r3(kn), r3(kr), r3(v))
    return out.reshape(batch * seq, groups * width)


def _norm_proj_kernel(x_ref, g_ref, w_ref, o_ref):
    h = _rms(x_ref[...], g_ref[...]).astype(BF16)
    o_ref[...] = jnp.dot(h, w_ref[...], preferred_element_type=F32).astype(o_ref.dtype)


def _sb_in_proj(x, g, w_in, *, tm=512):
    n, d = x.shape
    nq = C_HEADS * HEAD_DIM
    w = jnp.concatenate([w_in[:, :nq] * HEAD_DIM ** -0.5, w_in[:, nq:]], axis=1).astype(BF16)
    return pl.pallas_call(
        _norm_proj_kernel,
        grid=(n // tm,),
        in_specs=[pl.BlockSpec((tm, d), lambda i: (i, 0)), _const_spec((1, d)), _const_spec(w.shape)],
        out_specs=pl.BlockSpec((tm, w.shape[1]), lambda i: (i, 0)),
        out_shape=jax.ShapeDtypeStruct((n, w.shape[1]), BF16),
        compiler_params=_params(("parallel",)),
        name="sb_in_proj",
    )(x, g.reshape(1, d), w)


def _sb_t_kernel(q_ref, k_ref, v_ref, tri_ref, o_ref, qs_ref, vt_ref, *, tq, tk, cw, skew):
    i = pl.program_id(2)
    n_lb = q_ref.shape[1] // LANES
    n_cq = tq // cw
    lo = _low_head_lanes()
    strict = lax.broadcasted_iota(jnp.int32, (cw, cw), 0) < lax.broadcasted_iota(jnp.int32, (cw, cw), 1)
    for lb in range(n_lb):
        x = q_ref[:, lb * LANES:(lb + 1) * LANES]
        zero = jnp.zeros_like(x)
        qs_ref[lb, 0] = jnp.where(lo, x, zero)
        qs_ref[lb, 1] = jnp.where(lo, zero, x)

    @pl.when(i == 0)
    def _():
        for lb in range(n_lb):
            vt = v_ref[:, lb * LANES:(lb + 1) * LANES].astype(F32).T
            for e in range(2):
                vt_ref[lb, e] = vt[e * HEAD_DIM:(e + 1) * HEAD_DIM].astype(BF16)

    chains = [(lb, e, c) for lb in range(n_lb) for e in range(2) for c in range(n_cq)]

    def step(kb, carry, diagonal):
        base = pl.multiple_of(kb * (tq if diagonal else tk), tk)
        width = lambda c: (c + 1) * cw if diagonal else tk

        def masked(x, c, fill):
            last = jnp.where(strict, x[-cw:], jnp.full((cw, cw), fill, x.dtype))
            return last if c == 0 else jnp.concatenate([x[:-cw], last], axis=0)

        n_ch = len(chains)
        zs, incls, new_done, new_acc = ([None] * n_ch for _ in range(4))

        def scores(n):
            lb, e, c = chains[n]
            zs[n] = lax.dot_general(k_ref[pl.ds(base, width(c)), lb * LANES:(lb + 1) * LANES],
                                    qs_ref[lb, e, c * cw:(c + 1) * cw, :], _NT, preferred_element_type=F32)

        def cumulative(n):
            lb, e, c = chains[n]
            zb = zs[n].astype(BF16)
            sp = jnp.maximum(zb, 0) + jnp.log(1 + jnp.exp(-jnp.abs(zb)))
            if diagonal:
                sp = masked(sp, c, 0)
            incls[n] = jnp.dot(tri_ref[:width(c), :width(c)], sp, preferred_element_type=F32)

        def outputs(n):
            lb, e, c = chains[n]
            a = jnp.exp(zs[n] - incls[n] - carry[0][n])
            if diagonal:
                a = masked(a, c, 0)
            new_acc[n] = carry[1][n] + jnp.dot(vt_ref[lb, e, :, pl.ds(base, width(c))], a.astype(BF16),
                                               preferred_element_type=F32)
            new_done[n] = carry[0][n] + incls[n][0:1, :]

        for n in range(n_ch + 2 * skew):
            for stage, m in ((scores, n), (cumulative, n - skew), (outputs, n - 2 * skew)):
                if 0 <= m < n_ch:
                    stage(m)
        return tuple(new_done), tuple(new_acc)

    init = (tuple(jnp.zeros((1, cw), F32) for _ in chains), tuple(jnp.zeros((HEAD_DIM, cw), F32) for _ in chains))
    carry = step(i, init, True)
    n_below = i * (tq // tk)
    _, acc = lax.fori_loop(0, n_below, lambda t, c: step(n_below - 1 - t, c, False), carry)
    for lb in range(n_lb):
        for c in range(n_cq):
            out_t = jnp.concatenate([acc[(lb * 2 + e) * n_cq + c] for e in range(2)], axis=0)
            o_ref[c * cw:(c + 1) * cw, lb * LANES:(lb + 1) * LANES] = out_t.T.astype(BF16)


def _sb_attn(qkv, batch, seq, *, tq=512, tk=256, n_lb=2, skew=8):
    width = n_lb * LANES
    groups = C_HEADS * HEAD_DIM // width
    q3 = qkv.reshape(batch, seq, qkv.shape[1])
    tri = jnp.asarray(np.triu(np.ones((tq, tq), np.float32)), BF16)
    out = pl.pallas_call(
        functools.partial(_sb_t_kernel, tq=tq, tk=tk, cw=2 * LANES, skew=skew),
        grid=(batch, groups, seq // tq),
        in_specs=[pl.BlockSpec((None, tq, width), lambda b, g, i: (b, i, g)),
                  pl.BlockSpec((None, seq, width), lambda b, g, i: (b, 0, groups + g)),
                  pl.BlockSpec((None, seq, width), lambda b, g, i: (b, 0, 2 * groups + g)),
                  pl.BlockSpec((tq, tq), lambda b, g, i: (0, 0))],
        out_specs=pl.BlockSpec((None, tq, width), lambda b, g, i: (b, i, g)),
        out_shape=jax.ShapeDtypeStruct((batch, seq, groups * width), BF16),
        scratch_shapes=[pltpu.VMEM((n_lb, 2, tq, LANES), BF16), pltpu.VMEM((n_lb, 2, HEAD_DIM, seq), BF16)],
        compiler_params=_params(("parallel", "parallel", "arbitrary")),
        name="sb_attn",
    )(q3, q3, q3, tri)
    return out.reshape(batch * seq, groups * width)


def kernel(x, ffn_norm_g, mix_norm_g, ffn_w_gate, ffn_w_up, ffn_w_down, ab_w_in, mla_q_norm_g,
           mla_w_uq, mla_kv_norm_g, mla_w_ukv, ab_w_out, sb_w_in, sb_w_out, final_norm_g):
    batch, seq, d = x.shape
    depth = ffn_norm_g.shape[0]
    h = x.reshape(batch * seq, d)
    bf = lambda w: w.astype(BF16)

    wg, wu, wd = bf(ffn_w_gate), bf(ffn_w_up), bf(ffn_w_down)

    def ffn(h, pre, i, s, final_g=None):
        return _ffn(h, pre, ffn_norm_g[i, s], wg, wu, wd, (i, s), final_g)

    for i in range(depth):
        h = ffn(h, [], i, 0)
        if i % 2 == 0:
            e = i // 2
            a_qkv, qn, qr, kn, kr, v = _ab_in_proj(h, seq, mix_norm_g[i], ab_w_in[e], mla_q_norm_g[e],
                                                   mla_w_uq[e], mla_kv_norm_g[e], mla_w_ukv[e])
            o_a = _dilated_attn(a_qkv, batch, seq)
            o_b = _mla_attn(qn, qr, kn, kr, v, batch, seq)
            w_out = bf(ab_w_out[e])
            na = o_a.shape[1]
            pre = [(o_a, w_out[:na]), (o_b, w_out[na:])]
        else:
            o = i // 2
            qkv = _sb_in_proj(h, mix_norm_g[i], sb_w_in[o])
            pre = [(_sb_attn(qkv, batch, seq), bf(sb_w_out[o]))]
        h = ffn(h, pre, i, 1, final_norm_g if i == depth - 1 else None)
    return h.reshape(batch, seq, d)
```

```python
import functools

import numpy as np
import jax
import jax.numpy as jnp
from jax import lax
from jax.experimental import pallas as pl
from jax.experimental.pallas import tpu as pltpu

F32 = jnp.float32
BF16 = jnp.bfloat16

NORM_EPS = 1e-6
ROPE_THETA = 10000.0
LANES = 128
HEAD_DIM = 64
BLOCK = 128
A_HEADS = 8
A_BRANCHES = ((128, 1), (512, 4), (2048, 16))
B_HEADS = 8
B_NOPE = 64
B_ROPE = 32
B_Q_RANK = 256
B_KV_RANK = 128
C_HEADS = 16
NEG_BIG = -1e30
LOG2E = 1.4426950408889634
VMEM_LIMIT = 58 * 1024 * 1024

_NT = (((1,), (1,)), ((), ()))


def _rms(x, g):
    return x * lax.rsqrt(jnp.mean(x * x, axis=-1, keepdims=True) + NORM_EPS) * g


def _const_spec(shape):
    nd = len(shape)
    return pl.BlockSpec(shape, lambda *_: (0,) * nd, pipeline_mode=pl.Buffered(1))


def _params(sem):
    return pltpu.CompilerParams(dimension_semantics=sem, vmem_limit_bytes=VMEM_LIMIT)


def _ffn_kernel(*refs, n_pre, final_norm, tf):
    x_ref = refs[0]
    pre = refs[1:1 + 2 * n_pre]
    g_ref, wg_ref, wu_ref, wd_ref = refs[1 + 2 * n_pre:5 + 2 * n_pre]
    rest = refs[5 + 2 * n_pre:]
    fg_ref = rest[0] if final_norm else None
    o_ref = rest[-1]

    x = x_ref[...]
    for p in range(n_pre):
        x = x + jnp.dot(pre[2 * p][...], pre[2 * p + 1][...], preferred_element_type=F32)
    h = _rms(x, g_ref[...]).astype(BF16)
    d_ff = wg_ref.shape[1]
    acc = None
    for c in range(d_ff // tf):
        sl = slice(c * tf, (c + 1) * tf)
        gate = jnp.dot(h, wg_ref[:, sl], preferred_element_type=F32)
        up = jnp.dot(h, wu_ref[:, sl], preferred_element_type=F32)
        a = (gate * jax.nn.sigmoid(gate) * up).astype(BF16)
        d = jnp.dot(a, wd_ref[sl, :], preferred_element_type=F32)
        acc = d if acc is None else acc + d
    y = x + 0.5 * acc
    if final_norm:
        y = _rms(y, fg_ref[...])
    o_ref[...] = y


def _ffn(x, pre, g, wg, wu, wd, which, final_g=None, *, tm=512, tf=256):
    n, d = x.shape
    row = lambda i: (i, 0)
    picked = lambda w: pl.BlockSpec((None, None) + w.shape[2:], lambda i: which + (0, 0),
                                    pipeline_mode=pl.Buffered(1))
    args = [x]
    specs = [pl.BlockSpec((tm, d), row)]
    for o, w in pre:
        args += [o, w]
        specs += [pl.BlockSpec((tm, o.shape[1]), row), _const_spec(w.shape)]
    args += [g.reshape(1, d), wg, wu, wd]
    specs += [_const_spec((1, d)), picked(wg), picked(wu), picked(wd)]
    if final_g is not None:
        args.append(final_g.reshape(1, d))
        specs.append(_const_spec((1, d)))
    return pl.pallas_call(
        functools.partial(_ffn_kernel, n_pre=len(pre), final_norm=final_g is not None, tf=tf),
        grid=(n // tm,),
        in_specs=specs,
        out_specs=pl.BlockSpec((tm, d), row),
        out_shape=jax.ShapeDtypeStruct((n, d), F32),
        compiler_params=_params(("parallel",)),
        name="ffn",
    )(*args)


def _ab_in_kernel(x_ref, g_ref, win_ref, qg_ref, wuq_ref, kvg_ref, wukv_ref,
                  cq_ref, sq_ref, ck_ref, sk_ref,
                  a_ref, qn_ref, qr_ref, kn_ref, kr_ref, v_ref, *, scale):
    h = _rms(x_ref[...], g_ref[...]).astype(BF16)
    p = jnp.dot(h, win_ref[...], preferred_element_type=F32)
    na = a_ref.shape[1]
    a_ref[...] = p[:, :na]
    c_q = p[:, na:na + B_Q_RANK]
    c_kv = p[:, na + B_Q_RANK:na + B_Q_RANK + B_KV_RANK]
    k0 = na + B_Q_RANK + B_KV_RANK
    kr_ref[...] = (p[:, k0:k0 + LANES] * ck_ref[...] + p[:, k0 + LANES:k0 + 2 * LANES] * sk_ref[...]).astype(BF16)
    q = jnp.dot(_rms(c_q, qg_ref[...]).astype(BF16), wuq_ref[...], preferred_element_type=F32)
    nn = qn_ref.shape[1]
    nr = qr_ref.shape[1]
    qn_ref[...] = (q[:, :nn] * scale).astype(BF16)
    qr_ref[...] = ((q[:, nn:nn + nr] * cq_ref[...] + q[:, nn + nr:nn + 2 * nr] * sq_ref[...]) * scale).astype(BF16)
    kv = jnp.dot(_rms(c_kv, kvg_ref[...]).astype(BF16), wukv_ref[...], preferred_element_type=F32)
    kn_ref[...] = kv[:, :nn].astype(BF16)
    v_ref[...] = kv[:, nn:].astype(BF16)


def _rot_half_cols(w, width):
    k, n = w.shape
    w3 = w.reshape(k, n // width, 2, width // 2)
    return jnp.stack([-w3[:, :, 1], w3[:, :, 0]], axis=2).reshape(k, n)


def _ab_in_proj(x, seq, g, w_in, q_g, w_uq, kv_g, w_ukv, *, tm=512):
    n, d = x.shape
    a_cols = 3 * A_HEADS * HEAD_DIM
    a_scale = HEAD_DIM ** -0.5
    w_a = jnp.concatenate([w_in[:, :A_HEADS * HEAD_DIM] * a_scale, w_in[:, A_HEADS * HEAD_DIM:a_cols]], axis=1)
    w_lat = w_in[:, a_cols:a_cols + B_Q_RANK + B_KV_RANK]
    w_kr = w_in[:, a_cols + B_Q_RANK + B_KV_RANK:]
    reps = LANES // B_ROPE
    w_full = jnp.concatenate(
        [w_a, w_lat, jnp.tile(w_kr, (1, reps)), jnp.tile(_rot_half_cols(w_kr, B_ROPE), (1, reps))], axis=1).astype(BF16)
    uq = w_uq.reshape(B_Q_RANK, B_HEADS, B_NOPE + B_ROPE)
    uq_n = uq[:, :, :B_NOPE].reshape(B_Q_RANK, B_HEADS * B_NOPE)
    uq_r = uq[:, :, B_NOPE:].reshape(B_Q_RANK, B_HEADS * B_ROPE)
    w_uq_full = jnp.concatenate([uq_n, uq_r, _rot_half_cols(uq_r, B_ROPE)], axis=1).astype(BF16)
    ukv = w_ukv.reshape(B_KV_RANK, B_HEADS, 2, B_NOPE)
    w_ukv_full = jnp.concatenate([ukv[:, :, 0].reshape(B_KV_RANK, -1), ukv[:, :, 1].reshape(B_KV_RANK, -1)], axis=1).astype(BF16)
    inv = ROPE_THETA ** (-jnp.arange(0, B_ROPE, 2, dtype=F32) / B_ROPE)
    ang = jnp.arange(seq, dtype=F32)[:, None] * inv[None, :]
    cos2 = jnp.concatenate([jnp.cos(ang)] * 2, axis=1)
    sin2 = jnp.concatenate([jnp.sin(ang)] * 2, axis=1)
    cq, sq = jnp.tile(cos2, (1, B_HEADS)), jnp.tile(sin2, (1, B_HEADS))
    ck, sk = jnp.tile(cos2, (1, reps)), jnp.tile(sin2, (1, reps))

    row = lambda i: (i, 0)
    per_seq = seq // tm
    pos = lambda i: (i % per_seq, 0)
    nq_n = B_HEADS * B_NOPE
    nq_r = B_HEADS * B_ROPE
    outs = [(a_cols, F32), (nq_n, BF16), (nq_r, BF16), (nq_n, BF16), (LANES, BF16), (nq_n, BF16)]
    return pl.pallas_call(
        functools.partial(_ab_in_kernel, scale=(B_NOPE + B_ROPE) ** -0.5),
        grid=(n // tm,),
        in_specs=[pl.BlockSpec((tm, d), row), _const_spec((1, d)), _const_spec(w_full.shape),
                  _const_spec((1, B_Q_RANK)), _const_spec(w_uq_full.shape),
                  _const_spec((1, B_KV_RANK)), _const_spec(w_ukv_full.shape),
                  pl.BlockSpec((tm, nq_r), pos), pl.BlockSpec((tm, nq_r), pos),
                  pl.BlockSpec((tm, LANES), pos), pl.BlockSpec((tm, LANES), pos)],
        out_specs=[pl.BlockSpec((tm, c), row) for c, _ in outs],
        out_shape=[jax.ShapeDtypeStruct((n, c), dt) for c, dt in outs],
        compiler_params=_params(("parallel",)),
        name="ab_in_proj",
    )(x, g.reshape(1, d), w_full, q_g.reshape(1, -1), w_uq_full, kv_g.reshape(1, -1), w_ukv_full, cq, sq, ck, sk)


def _low_head_lanes():
    return lax.broadcasted_iota(jnp.int32, (1, LANES), 1) < HEAD_DIM


def _stack_heads(x, lo):
    zero = jnp.zeros_like(x)
    return jnp.concatenate([jnp.where(lo, x, zero), jnp.where(lo, zero, x)], axis=0)


def _pv(p, v, lo):
    r = p.shape[0] // 2
    zero = jnp.zeros_like(v)
    return (jnp.dot(p[:r], jnp.where(lo, v, zero), preferred_element_type=F32)
            + jnp.dot(p[r:], jnp.where(lo, zero, v), preferred_element_type=F32))


def _unstack(x, lo):
    r = x.shape[0] // 2
    return jnp.where(lo, x[:r], x[r:])


def _dilated_kernel(*refs, n_lb, unroll):
    q_refs, k_refs, v_refs = refs[:n_lb], refs[n_lb:2 * n_lb], refs[2 * n_lb:3 * n_lb]
    bias_ref, o_ref, qd, kd, vd, lwd, od, lws, os_ = refs[3 * n_lb:]
    seq = o_ref.shape[0]
    lo = _low_head_lanes()

    def run_branch(bi, nb, src_q, src_k, src_v, dst_lw, dst_o):
        def blocks(it, _):
            work = []
            for u in range(unroll):
                idx = it * unroll + u
                base = pl.multiple_of(idx * BLOCK, BLOCK)
                cur = pl.ds(base, BLOCK)
                prv = pl.ds(pl.multiple_of(jnp.maximum(base - BLOCK, 0), BLOCK), BLOCK)
                pen = jnp.where((idx % nb) == 0, NEG_BIG, 0.0).astype(F32)
                work += [(lb, cur, prv, pen) for lb in range(n_lb)]
            scores = []
            for lb, cur, prv, pen in work:
                qs = _stack_heads(src_q[lb][cur, :].astype(BF16), lo)
                s_c = (lax.dot_general(qs, src_k[lb][cur, :].astype(BF16), _NT, preferred_element_type=F32)
                       + bias_ref[bi, lb, :, BLOCK:])
                s_p = None
                if nb > 1:
                    s_p = (lax.dot_general(qs, src_k[lb][prv, :].astype(BF16), _NT, preferred_element_type=F32)
                           + bias_ref[bi, lb, :, :BLOCK] + pen)
                scores.append((s_c, s_p))
            probs = []
            for s_c, s_p in scores:
                m = jnp.max(s_c if s_p is None else jnp.maximum(s_c, s_p), axis=-1, keepdims=True)
                p_c = jnp.exp(s_c - m)
                p_p = None if s_p is None else jnp.exp(s_p - m)
                l = jnp.sum(p_c if s_p is None else p_c + p_p, axis=-1, keepdims=True)
                probs.append((m, l, p_c, p_p))
            for (lb, cur, prv, _), (m, l, p_c, p_p) in zip(work, probs):
                acc = _pv(p_c.astype(BF16), src_v[lb][cur, :].astype(BF16), lo)
                if nb > 1:
                    acc = acc + _pv(p_p.astype(BF16), src_v[lb][prv, :].astype(BF16), lo)
                l2 = _unstack(l, lo)
                dst_o[lb][cur, :] = acc * (1.0 / l2)
                dst_lw[lb][cur, :] = _unstack(m, lo) + jnp.log(l2)
            return 0

        lax.fori_loop(0, seq // BLOCK // unroll, blocks, 0)

    per_lb = lambda ref, *lead: [ref.at[(*lead, lb)] for lb in range(n_lb)]
    for bi, (_, dil) in enumerate(A_BRANCHES):
        sub = seq // dil
        nb = -(-sub // BLOCK)
        if dil == 1:
            run_branch(bi, nb, q_refs, k_refs, v_refs, per_lb(lws, bi), per_lb(os_, bi))
            continue
        for lb in range(n_lb):
            for r in range(dil):
                rows = pl.ds(r, sub, stride=dil)
                qd[lb, r * sub:(r + 1) * sub, :] = q_refs[lb][rows, :].astype(BF16)
                kd[lb, r * sub:(r + 1) * sub, :] = k_refs[lb][rows, :].astype(BF16)
                vd[lb, r * sub:(r + 1) * sub, :] = v_refs[lb][rows, :].astype(BF16)
        run_branch(bi, nb, per_lb(qd), per_lb(kd), per_lb(vd), per_lb(lwd), per_lb(od))
        for lb in range(n_lb):
            for r in range(dil):
                rows = pl.ds(r, sub, stride=dil)
                lws[bi, lb, rows, :] = lwd[lb, r * sub:(r + 1) * sub, :]
                os_[bi, lb, rows, :] = od[lb, r * sub:(r + 1) * sub, :]

    nbr = len(A_BRANCHES)

    def merge(idx, _):
        rows = pl.ds(pl.multiple_of(idx * BLOCK, BLOCK), BLOCK)
        for lb in range(n_lb):
            lw = [lws[i, lb, rows, :] for i in range(nbr)]
            top = functools.reduce(jnp.maximum, lw)
            w = [jnp.exp(x - top) for x in lw]
            num = sum(w[i] * os_[i, lb, rows, :] for i in range(nbr))
            o_ref[rows, lb * LANES:(lb + 1) * LANES] = (num / sum(w)).astype(BF16)
        return 0

    lax.fori_loop(0, seq // BLOCK, merge, 0)


def _alibi_bias():
    slopes = 2.0 ** (-8.0 * np.arange(1, A_HEADS + 1, dtype=np.float64) / A_HEADS)
    qi = np.arange(BLOCK)[:, None]
    kk = np.arange(2 * BLOCK)[None, :]
    rel = qi + BLOCK - kk
    out = np.empty((len(A_BRANCHES), A_HEADS, BLOCK, 2 * BLOCK), np.float32)
    for bi, (window, dil) in enumerate(A_BRANCHES):
        valid = (rel >= 0) & (rel <= window // dil)
        bias = -slopes[:, None, None] * (rel * dil)[None]
        out[bi] = np.where(valid[None], bias, NEG_BIG)
    return jnp.asarray(out.reshape(len(A_BRANCHES), A_HEADS // 2, 2 * BLOCK, 2 * BLOCK))


def _dilated_attn(a_qkv, batch, seq, *, n_lb=2, unroll=4):
    a3 = a_qkv.reshape(batch, seq, a_qkv.shape[1])
    width = n_lb * LANES
    groups = A_HEADS * HEAD_DIM // width
    bias = _alibi_bias()
    blk = lambda t, lb: pl.BlockSpec((None, seq, LANES), lambda g, b: (b, 0, (t * groups + g) * n_lb + lb))
    nbr = len(A_BRANCHES)
    out = pl.pallas_call(
        functools.partial(_dilated_kernel, n_lb=n_lb, unroll=unroll),
        grid=(groups, batch),
        in_specs=[blk(t, lb) for t in range(3) for lb in range(n_lb)]
                 + [pl.BlockSpec((nbr, n_lb, 2 * BLOCK, 2 * BLOCK), lambda g, b: (0, g, 0, 0))],
        out_specs=pl.BlockSpec((None, seq, width), lambda g, b: (b, 0, g)),
        out_shape=jax.ShapeDtypeStruct((batch, seq, groups * width), BF16),
        scratch_shapes=[pltpu.VMEM((n_lb, seq, LANES), BF16)] * 3 + [pltpu.VMEM((n_lb, seq, LANES), F32)] * 2
                       + [pltpu.VMEM((nbr, n_lb, seq, LANES), F32)] * 2,
        compiler_params=_params(("parallel", "parallel")),
        name="dilated_attn",
    )(*([a3] * (3 * n_lb)), bias)
    return out.reshape(batch * seq, groups * width)


V_ROWS = 80


def _mla_t_kernel(qn_ref, qr_ref, kn_ref, kr_ref, v_ref, o_ref, qs_ref, vt_ref, *, tq, tk, cw, skew):
    i = pl.program_id(2)
    n_lb = qn_ref.shape[1] // LANES
    n_cq = tq // cw
    lane = lax.broadcasted_iota(jnp.int32, (1, LANES), 1)
    lo = lane < HEAD_DIM
    causal = lax.broadcasted_iota(jnp.int32, (cw, cw), 0) <= lax.broadcasted_iota(jnp.int32, (cw, cw), 1)

    qr = qr_ref[...]
    for lb in range(n_lb):
        qn = qn_ref[:, lb * LANES:(lb + 1) * LANES]
        for e in range(2):
            slot = 2 * lb + e
            in_slot = (lane >= slot * B_ROPE) & (lane < (slot + 1) * B_ROPE)
            qs_ref[lb, e, :, :LANES] = jnp.where(lo if e == 0 else ~lo, qn, jnp.zeros_like(qn))
            qs_ref[lb, e, :, LANES:] = jnp.where(in_slot, qr, jnp.zeros_like(qr))

    @pl.when(i == 0)
    def _():
        for lb in range(n_lb):
            vt = v_ref[:, lb * LANES:(lb + 1) * LANES].astype(F32).T
            for e in range(2):
                vt_ref[lb, e, :HEAD_DIM, :] = vt[e * HEAD_DIM:(e + 1) * HEAD_DIM].astype(BF16)
                vt_ref[lb, e, HEAD_DIM:, :] = jnp.ones((V_ROWS - HEAD_DIM, vt.shape[1]), BF16)

    chains = [(lb, e, c) for lb in range(n_lb) for e in range(2) for c in range(n_cq)]

    def step(kb, carry, diagonal):
        base = pl.multiple_of(kb * (tq if diagonal else tk), tk)
        width = lambda c: (c + 1) * cw if diagonal else tk
        n_ch = len(chains)
        ss, new_m, alphas, ps, new_acc = ([None] * n_ch for _ in range(5))

        def scores(n):
            lb, e, c = chains[n]
            rows = pl.ds(base, width(c))
            k = jnp.concatenate([kn_ref[rows, lb * LANES:(lb + 1) * LANES], kr_ref[rows, :]], axis=1)
            ss[n] = lax.dot_general(k, qs_ref[lb, e, c * cw:(c + 1) * cw, :], _NT, preferred_element_type=F32)

        def probabilities(n):
            lb, e, c = chains[n]
            s = ss[n]
            if diagonal:
                last = jnp.where(causal, s[-cw:], NEG_BIG)
                s = last if c == 0 else jnp.concatenate([s[:-cw], last], axis=0)
            new_m[n] = jnp.maximum(carry[0][n], jnp.max(s, axis=0, keepdims=True))
            alphas[n] = jnp.exp(carry[0][n] - new_m[n])
            ps[n] = jnp.exp(s - new_m[n]).astype(BF16)

        def outputs(n):
            lb, e, c = chains[n]
            vt = vt_ref[lb, e, :, pl.ds(base, width(c))]
            new_acc[n] = alphas[n] * carry[1][n] + jnp.dot(vt, ps[n], preferred_element_type=F32)

        for n in range(n_ch + 2 * skew):
            for stage, m in ((scores, n), (probabilities, n - skew), (outputs, n - 2 * skew)):
                if 0 <= m < n_ch:
                    stage(m)
        return tuple(new_m), tuple(new_acc)

    init = (tuple(jnp.full((1, cw), NEG_BIG, F32) for _ in chains),
            tuple(jnp.zeros((V_ROWS, cw), F32) for _ in chains))
    carry = lax.fori_loop(0, i * (tq // tk), functools.partial(step, diagonal=False), init)
    _, acc = step(i, carry, True)
    for lb in range(n_lb):
        for c in range(n_cq):
            heads = [acc[(lb * 2 + e) * n_cq + c] for e in range(2)]
            out_t = jnp.concatenate([a[:HEAD_DIM] / a[HEAD_DIM:HEAD_DIM + 1] for a in heads], axis=0)
            o_ref[c * cw:(c + 1) * cw, lb * LANES:(lb + 1) * LANES] = out_t.T.astype(BF16)


def _mla_attn(qn, qr, kn, kr, v, batch, seq, *, tq=1024, tk=512, skew=8):
    width = (LANES // B_ROPE) * B_NOPE
    groups = qn.shape[1] // width
    r3 = lambda t: t.reshape(batch, seq, t.shape[1])
    out = pl.pallas_call(
        functools.partial(_mla_t_kernel, tq=tq, tk=tk, cw=2 * LANES, skew=skew),
        grid=(batch, groups, seq // tq),
        in_specs=[pl.BlockSpec((None, tq, width), lambda b, g, i: (b, i, g)),
                  pl.BlockSpec((None, tq, LANES), lambda b, g, i: (b, i, g)),
                  pl.BlockSpec((None, seq, width), lambda b, g, i: (b, 0, g)),
                  pl.BlockSpec((None, seq, LANES), lambda b, g, i: (b, 0, 0)),
                  pl.BlockSpec((None, seq, width), lambda b, g, i: (b, 0, g))],
        out_specs=pl.BlockSpec((None, tq, width), lambda b, g, i: (b, i, g)),
        out_shape=jax.ShapeDtypeStruct((batch, seq, groups * width), BF16),
        scratch_shapes=[pltpu.VMEM((width // LANES, 2, tq, 2 * LANES), BF16),
                        pltpu.VMEM((width // LANES, 2, V_ROWS, seq), BF16)],
        compiler_params=_params(("parallel", "parallel", "arbitrary")),
        name="mla_attn",
    )(r3(qn), r3(qr), r3(kn), r3(kr), r3(v))
    return out.reshape(batch * seq, groups * width)


def _norm_proj_kernel(x_ref, g_ref, w_ref, o_ref):
    h = _rms(x_ref[...], g_ref[...]).astype(BF16)
    o_ref[...] = jnp.dot(h, w_ref[...], preferred_element_type=F32).astype(o_ref.dtype)


def _sb_in_proj(x, g, w_in, *, tm=512):
    n, d = x.shape
    nq = C_HEADS * HEAD_DIM
    w = jnp.concatenate([w_in[:, :nq] * HEAD_DIM ** -0.5, w_in[:, nq:]], axis=1).astype(BF16)
    return pl.pallas_call(
        _norm_proj_kernel,
        grid=(n // tm,),
        in_specs=[pl.BlockSpec((tm, d), lambda i: (i, 0)), _const_spec((1, d)), _const_spec(w.shape)],
        out_specs=pl.BlockSpec((tm, w.shape[1]), lambda i: (i, 0)),
        out_shape=jax.ShapeDtypeStruct((n, w.shape[1]), BF16),
        compiler_params=_params(("parallel",)),
        name="sb_in_proj",
    )(x, g.reshape(1, d), w)


def _sb_t_kernel(q_ref, k_ref, v_ref, tri_ref, o_ref, qs_ref, vt_ref, *, tq, tk, cw, skew):
    i = pl.program_id(2)
    n_lb = q_ref.shape[1] // LANES
    n_cq = tq // cw
    lo = _low_head_lanes()
    strict = lax.broadcasted_iota(jnp.int32, (cw, cw), 0) < lax.broadcasted_iota(jnp.int32, (cw, cw), 1)
    for lb in range(n_lb):
        x = q_ref[:, lb * LANES:(lb + 1) * LANES]
        zero = jnp.zeros_like(x)
        qs_ref[lb, 0] = jnp.where(lo, x, zero)
        qs_ref[lb, 1] = jnp.where(lo, zero, x)

    @pl.when(i == 0)
    def _():
        for lb in range(n_lb):
            vt = v_ref[:, lb * LANES:(lb + 1) * LANES].astype(F32).T
            for e in range(2):
                vt_ref[lb, e] = vt[e * HEAD_DIM:(e + 1) * HEAD_DIM].astype(BF16)

    chains = [(lb, e, c) for lb in range(n_lb) for e in range(2) for c in range(n_cq)]

    def step(kb, carry, diagonal):
        base = pl.multiple_of(kb * (tq if diagonal else tk), tk)
        width = lambda c: (c + 1) * cw if diagonal else tk

        def masked(x, c, fill):
            last = jnp.where(strict, x[-cw:], jnp.full((cw, cw), fill, x.dtype))
            return last if c == 0 else jnp.concatenate([x[:-cw], last], axis=0)

        n_ch = len(chains)
        zs, incls, new_done, new_acc = ([None] * n_ch for _ in range(4))

        def scores(n):
            lb, e, c = chains[n]
            zs[n] = lax.dot_general(k_ref[pl.ds(base, width(c)), lb * LANES:(lb + 1) * LANES],
                                    qs_ref[lb, e, c * cw:(c + 1) * cw, :], _NT, preferred_element_type=F32)

        def cumulative(n):
            lb, e, c = chains[n]
            zb = zs[n].astype(BF16)
            sp = jnp.maximum(zb, 0) + jnp.log(1 + jnp.exp(-jnp.abs(zb)))
            if diagonal:
                sp = masked(sp, c, 0)
            incls[n] = jnp.dot(tri_ref[:width(c), :width(c)], sp, preferred_element_type=F32)

        def outputs(n):
            lb, e, c = chains[n]
            a = jnp.exp(zs[n] - incls[n] - carry[0][n])
            if diagonal:
                a = masked(a, c, 0)
            new_acc[n] = carry[1][n] + jnp.dot(vt_ref[lb, e, :, pl.ds(base, width(c))], a.astype(BF16),
                                               preferred_element_type=F32)
            new_done[n] = carry[0][n] + incls[n][0:1, :]

        for n in range(n_ch + 2 * skew):
            for stage, m in ((scores, n), (cumulative, n - skew), (outputs, n - 2 * skew)):
                if 0 <= m < n_ch:
                    stage(m)
        return tuple(new_done), tuple(new_acc)

    init = (tuple(jnp.zeros((1, cw), F32) for _ in chains), tuple(jnp.zeros((HEAD_DIM, cw), F32) for _ in chains))
    carry = step(i, init, True)
    n_below = i * (tq // tk)
    _, acc = lax.fori_loop(0, n_below, lambda t, c: step(n_below - 1 - t, c, False), carry)
    for lb in range(n_lb):
        for c in range(n_cq):
            out_t = jnp.concatenate([acc[(lb * 2 + e) * n_cq + c] for e in range(2)], axis=0)
            o_ref[c * cw:(c + 1) * cw, lb * LANES:(lb + 1) * LANES] = out_t.T.astype(BF16)


def _sb_attn(qkv, batch, seq, *, tq=512, tk=256, n_lb=2, skew=8):
    width = n_lb * LANES
    groups = C_HEADS * HEAD_DIM // width
    q3 = qkv.reshape(batch, seq, qkv.shape[1])
    tri = jnp.asarray(np.triu(np.ones((tq, tq), np.float32)), BF16)
    out = pl.pallas_call(
        functools.partial(_sb_t_kernel, tq=tq, tk=tk, cw=2 * LANES, skew=skew),
        grid=(batch, groups, seq // tq),
        in_specs=[pl.BlockSpec((None, tq, width), lambda b, g, i: (b, i, g)),
                  pl.BlockSpec((None, seq, width), lambda b, g, i: (b, 0, groups + g)),
                  pl.BlockSpec((None, seq, width), lambda b, g, i: (b, 0, 2 * groups + g)),
                  pl.BlockSpec((tq, tq), lambda b, g, i: (0, 0))],
        out_specs=pl.BlockSpec((None, tq, width), lambda b, g, i: (b, i, g)),
        out_shape=jax.ShapeDtypeStruct((batch, seq, groups * width), BF16),
        scratch_shapes=[pltpu.VMEM((n_lb, 2, tq, LANES), BF16), pltpu.VMEM((n_lb, 2, HEAD_DIM, seq), BF16)],
        compiler_params=_params(("parallel", "parallel", "arbitrary")),
        name="sb_attn",
    )(q3, q3, q3, tri)
    return out.reshape(batch * seq, groups * width)


def kernel(x, ffn_norm_g, mix_norm_g, ffn_w_gate, ffn_w_up, ffn_w_down, ab_w_in, mla_q_norm_g,
           mla_w_uq, mla_kv_norm_g, mla_w_ukv, ab_w_out, sb_w_in, sb_w_out, final_norm_g):
    batch, seq, d = x.shape
    depth = ffn_norm_g.shape[0]
    h = x.reshape(batch * seq, d)
    bf = lambda w: w.astype(BF16)

    wg, wu, wd = bf(ffn_w_gate), bf(ffn_w_up), bf(ffn_w_down)

    def ffn(h, pre, i, s, final_g=None):
        return _ffn(h, pre, ffn_norm_g[i, s], wg, wu, wd, (i, s), final_g)

    for i in range(depth):
        h = ffn(h, [], i, 0)
        if i % 2 == 0:
            e = i // 2
            a_qkv, qn, qr, kn, kr, v = _ab_in_proj(h, seq, mix_norm_g[i], ab_w_in[e], mla_q_norm_g[e],
                                                   mla_w_uq[e], mla_kv_norm_g[e], mla_w_ukv[e])
            o_a = _dilated_attn(a_qkv, batch, seq)
            o_b = _mla_attn(qn, qr, kn, kr, v, batch, seq)
            w_out = bf(ab_w_out[e])
            na = o_a.shape[1]
            pre = [(o_a, w_out[:na]), (o_b, w_out[na:])]
        else:
            o = i // 2
            qkv = _sb_in_proj(h, mix_norm_g[i], sb_w_in[o])
            pre = [(_sb_attn(qkv, batch, seq), bf(sb_w_out[o]))]
        h = ffn(h, pre, i, 1, final_norm_g if i == depth - 1 else None)
    return h.reshape(batch, seq, d)
```

```python
import functools

import numpy as np
import jax
import jax.numpy as jnp
from jax import lax
from jax.experimental import pallas as pl
from jax.experimental.pallas import tpu as pltpu

F32 = jnp.float32
BF16 = jnp.bfloat16

NORM_EPS = 1e-6
ROPE_THETA = 10000.0
LANES = 128
HEAD_DIM = 64
BLOCK = 128
A_HEADS = 8
A_BRANCHES = ((128, 1), (512, 4), (2048, 16))
B_HEADS = 8
B_NOPE = 64
B_ROPE = 32
B_Q_RANK = 256
B_KV_RANK = 128
C_HEADS = 16
NEG_BIG = -1e30
LOG2E = 1.4426950408889634
VMEM_LIMIT = 58 * 1024 * 1024

_NT = (((1,), (1,)), ((), ()))


def _rms(x, g):
    return x * lax.rsqrt(jnp.mean(x * x, axis=-1, keepdims=True) + NORM_EPS) * g


def _const_spec(shape):
    nd = len(shape)
    return pl.BlockSpec(shape, lambda *_: (0,) * nd, pipeline_mode=pl.Buffered(1))


def _params(sem):
    return pltpu.CompilerParams(dimension_semantics=sem, vmem_limit_bytes=VMEM_LIMIT)


def _ffn_kernel(*refs, n_pre, final_norm, tf):
    x_ref = refs[0]
    pre = refs[1:1 + 2 * n_pre]
    g_ref, wg_ref, wu_ref, wd_ref = refs[1 + 2 * n_pre:5 + 2 * n_pre]
    rest = refs[5 + 2 * n_pre:]
    fg_ref = rest[0] if final_norm else None
    o_ref = rest[-1]

    x = x_ref[...]
    for p in range(n_pre):
        x = x + jnp.dot(pre[2 * p][...], pre[2 * p + 1][...], preferred_element_type=F32)
    h = _rms(x, g_ref[...]).astype(BF16)
    d_ff = wg_ref.shape[1]
    acc = None
    for c in range(d_ff // tf):
        sl = slice(c * tf, (c + 1) * tf)
        gate = jnp.dot(h, wg_ref[:, sl], preferred_element_type=F32)
        up = jnp.dot(h, wu_ref[:, sl], preferred_element_type=F32)
        a = (gate * jax.nn.sigmoid(gate) * up).astype(BF16)
        d = jnp.dot(a, wd_ref[sl, :], preferred_element_type=F32)
        acc = d if acc is None else acc + d
    y = x + 0.5 * acc
    if final_norm:
        y = _rms(y, fg_ref[...])
    o_ref[...] = y


def _ffn(x, pre, g, wg, wu, wd, which, final_g=None, *, tm=512, tf=256):
    n, d = x.shape
    row = lambda i: (i, 0)
    picked = lambda w: pl.BlockSpec((None, None) + w.shape[2:], lambda i: which + (0, 0),
                                    pipeline_mode=pl.Buffered(1))
    args = [x]
    specs = [pl.BlockSpec((tm, d), row)]
    for o, w in pre:
        args += [o, w]
        specs += [pl.BlockSpec((tm, o.shape[1]), row), _const_spec(w.shape)]
    args += [g.reshape(1, d), wg, wu, wd]
    specs += [_const_spec((1, d)), picked(wg), picked(wu), picked(wd)]
    if final_g is not None:
        args.append(final_g.reshape(1, d))
        specs.append(_const_spec((1, d)))
    return pl.pallas_call(
        functools.partial(_ffn_kernel, n_pre=len(pre), final_norm=final_g is not None, tf=tf),
        grid=(n // tm,),
        in_specs=specs,
        out_specs=pl.BlockSpec((tm, d), row),
        out_shape=jax.ShapeDtypeStruct((n, d), F32),
        compiler_params=_params(("parallel",)),
        name="ffn",
    )(*args)


def _ab_in_kernel(x_ref, g_ref, win_ref, qg_ref, wuq_ref, kvg_ref, wukv_ref,
                  cq_ref, sq_ref, ck_ref, sk_ref,
                  a_ref, qn_ref, qr_ref, kn_ref, kr_ref, v_ref, *, scale):
    h = _rms(x_ref[...], g_ref[...]).astype(BF16)
    p = jnp.dot(h, win_ref[...], preferred_element_type=F32)
    na = a_ref.shape[1]
    a_ref[...] = p[:, :na]
    c_q = p[:, na:na + B_Q_RANK]
    c_kv = p[:, na + B_Q_RANK:na + B_Q_RANK + B_KV_RANK]
    k0 = na + B_Q_RANK + B_KV_RANK
    kr_ref[...] = (p[:, k0:k0 + LANES] * ck_ref[...] + p[:, k0 + LANES:k0 + 2 * LANES] * sk_ref[...]).astype(BF16)
    q = jnp.dot(_rms(c_q, qg_ref[...]).astype(BF16), wuq_ref[...], preferred_element_type=F32)
    nn = qn_ref.shape[1]
    nr = qr_ref.shape[1]
    qn_ref[...] = (q[:, :nn] * scale).astype(BF16)
    qr_ref[...] = ((q[:, nn:nn + nr] * cq_ref[...] + q[:, nn + nr:nn + 2 * nr] * sq_ref[...]) * scale).astype(BF16)
    kv = jnp.dot(_rms(c_kv, kvg_ref[...]).astype(BF16), wukv_ref[...], preferred_element_type=F32)
    kn_ref[...] = kv[:, :nn].astype(BF16)
    v_ref[...] = kv[:, nn:].astype(BF16)


def _rot_half_cols(w, width):
    k, n = w.shape
    w3 = w.reshape(k, n // width, 2, width // 2)
    return jnp.stack([-w3[:, :, 1], w3[:, :, 0]], axis=2).reshape(k, n)


def _ab_in_proj(x, seq, g, w_in, q_g, w_uq, kv_g, w_ukv, *, tm=512):
    n, d = x.shape
    a_cols = 3 * A_HEADS * HEAD_DIM
    a_scale = HEAD_DIM ** -0.5
    w_a = jnp.concatenate([w_in[:, :A_HEADS * HEAD_DIM] * a_scale, w_in[:, A_HEADS * HEAD_DIM:a_cols]], axis=1)
    w_lat = w_in[:, a_cols:a_cols + B_Q_RANK + B_KV_RANK]
    w_kr = w_in[:, a_cols + B_Q_RANK + B_KV_RANK:]
    reps = LANES // B_ROPE
    w_full = jnp.concatenate(
        [w_a, w_lat, jnp.tile(w_kr, (1, reps)), jnp.tile(_rot_half_cols(w_kr, B_ROPE), (1, reps))], axis=1).astype(BF16)
    uq = w_uq.reshape(B_Q_RANK, B_HEADS, B_NOPE + B_ROPE)
    uq_n = uq[:, :, :B_NOPE].reshape(B_Q_RANK, B_HEADS * B_NOPE)
    uq_r = uq[:, :, B_NOPE:].reshape(B_Q_RANK, B_HEADS * B_ROPE)
    w_uq_full = jnp.concatenate([uq_n, uq_r, _rot_half_cols(uq_r, B_ROPE)], axis=1).astype(BF16)
    ukv = w_ukv.reshape(B_KV_RANK, B_HEADS, 2, B_NOPE)
    w_ukv_full = jnp.concatenate([ukv[:, :, 0].reshape(B_KV_RANK, -1), ukv[:, :, 1].reshape(B_KV_RANK, -1)], axis=1).astype(BF16)
    inv = ROPE_THETA ** (-jnp.arange(0, B_ROPE, 2, dtype=F32) / B_ROPE)
    ang = jnp.arange(seq, dtype=F32)[:, None] * inv[None, :]
    cos2 = jnp.concatenate([jnp.cos(ang)] * 2, axis=1)
    sin2 = jnp.concatenate([jnp.sin(ang)] * 2, axis=1)
    cq, sq = jnp.tile(cos2, (1, B_HEADS)), jnp.tile(sin2, (1, B_HEADS))
    ck, sk = jnp.tile(cos2, (1, reps)), jnp.tile(sin2, (1, reps))

    row = lambda i: (i, 0)
    per_seq = seq // tm
    pos = lambda i: (i % per_seq, 0)
    nq_n = B_HEADS * B_NOPE
    nq_r = B_HEADS * B_ROPE
    outs = [(a_cols, F32), (nq_n, BF16), (nq_r, BF16), (nq_n, BF16), (LANES, BF16), (nq_n, BF16)]
    return pl.pallas_call(
        functools.partial(_ab_in_kernel, scale=(B_NOPE + B_ROPE) ** -0.5),
        grid=(n // tm,),
        in_specs=[pl.BlockSpec((tm, d), row), _const_spec((1, d)), _const_spec(w_full.shape),
                  _const_spec((1, B_Q_RANK)), _const_spec(w_uq_full.shape),
                  _const_spec((1, B_KV_RANK)), _const_spec(w_ukv_full.shape),
                  pl.BlockSpec((tm, nq_r), pos), pl.BlockSpec((tm, nq_r), pos),
                  pl.BlockSpec((tm, LANES), pos), pl.BlockSpec((tm, LANES), pos)],
        out_specs=[pl.BlockSpec((tm, c), row) for c, _ in outs],
        out_shape=[jax.ShapeDtypeStruct((n, c), dt) for c, dt in outs],
        compiler_params=_params(("parallel",)),
        name="ab_in_proj",
    )(x, g.reshape(1, d), w_full, q_g.reshape(1, -1), w_uq_full, kv_g.reshape(1, -1), w_ukv_full, cq, sq, ck, sk)


def _low_head_lanes():
    return lax.broadcasted_iota(jnp.int32, (1, LANES), 1) < HEAD_DIM


def _stack_heads(x, lo):
    zero = jnp.zeros_like(x)
    return jnp.concatenate([jnp.where(lo, x, zero), jnp.where(lo, zero, x)], axis=0)


def _pv(p, v, lo):
    r = p.shape[0] // 2
    zero = jnp.zeros_like(v)
    return (jnp.dot(p[:r], jnp.where(lo, v, zero), preferred_element_type=F32)
            + jnp.dot(p[r:], jnp.where(lo, zero, v), preferred_element_type=F32))


def _unstack(x, lo):
    r = x.shape[0] // 2
    return jnp.where(lo, x[:r], x[r:])


def _dilated_kernel(*refs, n_lb, unroll):
    q_refs, k_refs, v_refs = refs[:n_lb], refs[n_lb:2 * n_lb], refs[2 * n_lb:3 * n_lb]
    bias_ref, o_ref, qd, kd, vd, lwd, od, lws, os_ = refs[3 * n_lb:]
    seq = o_ref.shape[0]
    lo = _low_head_lanes()

    def run_branch(bi, nb, src_q, src_k, src_v, dst_lw, dst_o):
        def blocks(it, _):
            work = []
            for u in range(unroll):
                idx = it * unroll + u
                base = pl.multiple_of(idx * BLOCK, BLOCK)
                cur = pl.ds(base, BLOCK)
                prv = pl.ds(pl.multiple_of(jnp.maximum(base - BLOCK, 0), BLOCK), BLOCK)
                pen = jnp.where((idx % nb) == 0, NEG_BIG, 0.0).astype(F32)
                work += [(lb, cur, prv, pen) for lb in range(n_lb)]
            scores = []
            for lb, cur, prv, pen in work:
                qs = _stack_heads(src_q[lb][cur, :].astype(BF16), lo)
                s_c = (lax.dot_general(qs, src_k[lb][cur, :].astype(BF16), _NT, preferred_element_type=F32)
                       + bias_ref[bi, lb, :, BLOCK:])
                s_p = None
                if nb > 1:
                    s_p = (lax.dot_general(qs, src_k[lb][prv, :].astype(BF16), _NT, preferred_element_type=F32)
                           + bias_ref[bi, lb, :, :BLOCK] + pen)
                scores.append((s_c, s_p))
            probs = []
            for s_c, s_p in scores:
                m = jnp.max(s_c if s_p is None else jnp.maximum(s_c, s_p), axis=-1, keepdims=True)
                p_c = jnp.exp(s_c - m)
                p_p = None if s_p is None else jnp.exp(s_p - m)
                l = jnp.sum(p_c if s_p is None else p_c + p_p, axis=-1, keepdims=True)
                probs.append((m, l, p_c, p_p))
            for (lb, cur, prv, _), (m, l, p_c, p_p) in zip(work, probs):
                acc = _pv(p_c.astype(BF16), src_v[lb][cur, :].astype(BF16), lo)
                if nb > 1:
                    acc = acc + _pv(p_p.astype(BF16), src_v[lb][prv, :].astype(BF16), lo)
                l2 = _unstack(l, lo)
                dst_o[lb][cur, :] = acc * (1.0 / l2)
                dst_lw[lb][cur, :] = _unstack(m, lo) + jnp.log(l2)
            return 0

        lax.fori_loop(0, seq // BLOCK // unroll, blocks, 0)

    per_lb = lambda ref, *lead: [ref.at[(*lead, lb)] for lb in range(n_lb)]
    for bi, (_, dil) in enumerate(A_BRANCHES):
        sub = seq // dil
        nb = -(-sub // BLOCK)
        if dil == 1:
            run_branch(bi, nb, q_refs, k_refs, v_refs, per_lb(lws, bi), per_lb(os_, bi))
            continue
        for lb in range(n_lb):
            for r in range(dil):
                rows = pl.ds(r, sub, stride=dil)
                qd[lb, r * sub:(r + 1) * sub, :] = q_refs[lb][rows, :].astype(BF16)
                kd[lb, r * sub:(r + 1) * sub, :] = k_refs[lb][rows, :].astype(BF16)
                vd[lb, r * sub:(r + 1) * sub, :] = v_refs[lb][rows, :].astype(BF16)
        run_branch(bi, nb, per_lb(qd), per_lb(kd), per_lb(vd), per_lb(lwd), per_lb(od))
        for lb in range(n_lb):
            for r in range(dil):
                rows = pl.ds(r, sub, stride=dil)
                lws[bi, lb, rows, :] = lwd[lb, r * sub:(r + 1) * sub, :]
                os_[bi, lb, rows, :] = od[lb, r * sub:(r + 1) * sub, :]

    nbr = len(A_BRANCHES)

    def merge(idx, _):
        rows = pl.ds(pl.multiple_of(idx * BLOCK, BLOCK), BLOCK)
        for lb in range(n_lb):
            lw = [lws[i, lb, rows, :] for i in range(nbr)]
            top = functools.reduce(jnp.maximum, lw)
            w = [jnp.exp(x - top) for x in lw]
            num = sum(w[i] * os_[i, lb, rows, :] for i in range(nbr))
            o_ref[rows, lb * LANES:(lb + 1) * LANES] = (num / sum(w)).astype(BF16)
        return 0

    lax.fori_loop(0, seq // BLOCK, merge, 0)


def _alibi_bias():
    slopes = 2.0 ** (-8.0 * np.arange(1, A_HEADS + 1, dtype=np.float64) / A_HEADS)
    qi = np.arange(BLOCK)[:, None]
    kk = np.arange(2 * BLOCK)[None, :]
    rel = qi + BLOCK - kk
    out = np.empty((len(A_BRANCHES), A_HEADS, BLOCK, 2 * BLOCK), np.float32)
    for bi, (window, dil) in enumerate(A_BRANCHES):
        valid = (rel >= 0) & (rel <= window // dil)
        bias = -slopes[:, None, None] * (rel * dil)[None]
        out[bi] = np.where(valid[None], bias, NEG_BIG)
    return jnp.asarray(out.reshape(len(A_BRANCHES), A_HEADS // 2, 2 * BLOCK, 2 * BLOCK))


def _dilated_attn(a_qkv, batch, seq, *, n_lb=2, unroll=4):
    a3 = a_qkv.reshape(batch, seq, a_qkv.shape[1])
    width = n_lb * LANES
    groups = A_HEADS * HEAD_DIM // width
    bias = _alibi_bias()
    blk = lambda t, lb: pl.BlockSpec((None, seq, LANES), lambda g, b: (b, 0, (t * groups + g) * n_lb + lb))
    nbr = len(A_BRANCHES)
    out = pl.pallas_call(
        functools.partial(_dilated_kernel, n_lb=n_lb, unroll=unroll),
        grid=(groups, batch),
        in_specs=[blk(t, lb) for t in range(3) for lb in range(n_lb)]
                 + [pl.BlockSpec((nbr, n_lb, 2 * BLOCK, 2 * BLOCK), lambda g, b: (0, g, 0, 0))],
        out_specs=pl.BlockSpec((None, seq, width), lambda g, b: (b, 0, g)),
        out_shape=jax.ShapeDtypeStruct((batch, seq, groups * width), BF16),
        scratch_shapes=[pltpu.VMEM((n_lb, seq, LANES), BF16)] * 3 + [pltpu.VMEM((n_lb, seq, LANES), F32)] * 2
                       + [pltpu.VMEM((nbr, n_lb, seq, LANES), F32)] * 2,
        compiler_params=_params(("parallel", "parallel")),
        name="dilated_attn",
    )(*([a3] * (3 * n_lb)), bias)
    return out.reshape(batch * seq, groups * width)


V_ROWS = 80


def _mla_t_kernel(qn_ref, qr_ref, kn_ref, kr_ref, v_ref, o_ref, qs_ref, vt_ref, *, tq, tk, cw, skew):
    i = pl.program_id(2)
    n_lb = qn_ref.shape[1] // LANES
    n_cq = tq // cw
    lane = lax.broadcasted_iota(jnp.int32, (1, LANES), 1)
    lo = lane < HEAD_DIM
    causal = lax.broadcasted_iota(jnp.int32, (cw, cw), 0) <= lax.broadcasted_iota(jnp.int32, (cw, cw), 1)

    qr = qr_ref[...]
    for lb in range(n_lb):
        qn = qn_ref[:, lb * LANES:(lb + 1) * LANES]
        for e in range(2):
            slot = 2 * lb + e
            in_slot = (lane >= slot * B_ROPE) & (lane < (slot + 1) * B_ROPE)
            qs_ref[lb, e, :, :LANES] = jnp.where(lo if e == 0 else ~lo, qn, jnp.zeros_like(qn))
            qs_ref[lb, e, :, LANES:] = jnp.where(in_slot, qr, jnp.zeros_like(qr))

    @pl.when(i == 0)
    def _():
        for lb in range(n_lb):
            vt = v_ref[:, lb * LANES:(lb + 1) * LANES].astype(F32).T
            for e in range(2):
                vt_ref[lb, e, :HEAD_DIM, :] = vt[e * HEAD_DIM:(e + 1) * HEAD_DIM].astype(BF16)
                vt_ref[lb, e, HEAD_DIM:, :] = jnp.ones((V_ROWS - HEAD_DIM, vt.shape[1]), BF16)

    chains = [(lb, e, c) for lb in range(n_lb) for e in range(2) for c in range(n_cq)]

    def step(kb, carry, diagonal):
        base = pl.multiple_of(kb * (tq if diagonal else tk), tk)
        width = lambda c: (c + 1) * cw if diagonal else tk
        n_ch = len(chains)
        ss, new_m, alphas, ps, new_acc = ([None] * n_ch for _ in range(5))

        def scores(n):
            lb, e, c = chains[n]
            rows = pl.ds(base, width(c))
            k = jnp.concatenate([kn_ref[rows, lb * LANES:(lb + 1) * LANES], kr_ref[rows, :]], axis=1)
            ss[n] = lax.dot_general(k, qs_ref[lb, e, c * cw:(c + 1) * cw, :], _NT, preferred_element_type=F32)

        def probabilities(n):
            lb, e, c = chains[n]
            s = ss[n]
            if diagonal:
                last = jnp.where(causal, s[-cw:], NEG_BIG)
                s = last if c == 0 else jnp.concatenate([s[:-cw], last], axis=0)
            new_m[n] = jnp.maximum(carry[0][n], jnp.max(s, axis=0, keepdims=True))
            alphas[n] = jnp.exp(carry[0][n] - new_m[n])
            ps[n] = jnp.exp(s - new_m[n]).astype(BF16)

        def outputs(n):
            lb, e, c = chains[n]
            vt = vt_ref[lb, e, :, pl.ds(base, width(c))]
            new_acc[n] = alphas[n] * carry[1][n] + jnp.dot(vt, ps[n], preferred_element_type=F32)

        for n in range(n_ch + 2 * skew):
            for stage, m in ((scores, n), (probabilities, n - skew), (outputs, n - 2 * skew)):
                if 0 <= m < n_ch:
                    stage(m)
        return tuple(new_m), tuple(new_acc)

    init = (tuple(jnp.full((1, cw), NEG_BIG, F32) for _ in chains),
            tuple(jnp.zeros((V_ROWS, cw), F32) for _ in chains))
    carry = lax.fori_loop(0, i * (tq // tk), functools.partial(step, diagonal=False), init)
    _, acc = step(i, carry, True)
    for lb in range(n_lb):
        for c in range(n_cq):
            heads = [acc[(lb * 2 + e) * n_cq + c] for e in range(2)]
            out_t = jnp.concatenate([a[:HEAD_DIM] / a[HEAD_DIM:HEAD_DIM + 1] for a in heads], axis=0)
            o_ref[c * cw:(c + 1) * cw, lb * LANES:(lb + 1) * LANES] = out_t.T.astype(BF16)


def _mla_attn(qn, qr, kn, kr, v, batch, seq, *, tq=1024, tk=512, skew=8):
    width = (LANES // B_ROPE) * B_NOPE
    groups = qn.shape[1] // width
    r3 = lambda t: t.reshape(batch, seq, t.shape[1])
    out = pl.pallas_call(
        functools.partial(_mla_t_kernel, tq=tq, tk=tk, cw=2 * LANES, skew=skew),
        grid=(batch, groups, seq // tq),
        in_specs=[pl.BlockSpec((None, tq, width), lambda b, g, i: (b, i, g)),
                  pl.BlockSpec((None, tq, LANES), lambda b, g, i: (b, i, g)),
                  pl.BlockSpec((None, seq, width), lambda b, g, i: (b, 0, g)),
                  pl.BlockSpec((None, seq, LANES), lambda b, g, i: (b, 0, 0)),
                  pl.BlockSpec((None, seq, width), lambda b, g, i: (b, 0, g))],
        out_specs=pl.BlockSpec((None, tq, width), lambda b, g, i: (b, i, g)),
        out_shape=jax.ShapeDtypeStruct((batch, seq, groups * width), BF16),
        scratch_shapes=[pltpu.VMEM((width // LANES, 2, tq, 2 * LANES), BF16),
                        pltpu.VMEM((width // LANES, 2, V_ROWS, seq), BF16)],
        compiler_params=_params(("parallel", "parallel", "arbitrary")),
        name="mla_attn",
    )(r3(qn), r3(qr), r3(kn), r3(kr), r3(v))
    return out.reshape(batch * seq, groups * width)


def _norm_proj_kernel(x_ref, g_ref, w_ref, o_ref):
    h = _rms(x_ref[...], g_ref[...]).astype(BF16)
    o_ref[...] = jnp.dot(h, w_ref[...], preferred_element_type=F32).astype(o_ref.dtype)


def _sb_in_proj(x, g, w_in, *, tm=512):
    n, d = x.shape
    nq = C_HEADS * HEAD_DIM
    w = jnp.concatenate([w_in[:, :nq] * HEAD_DIM ** -0.5, w_in[:, nq:]], axis=1).astype(BF16)
    return pl.pallas_call(
        _norm_proj_kernel,
        grid=(n // tm,),
        in_specs=[pl.BlockSpec((tm, d), lambda i: (i, 0)), _const_spec((1, d)), _const_spec(w.shape)],
        out_specs=pl.BlockSpec((tm, w.shape[1]), lambda i: (i, 0)),
        out_shape=jax.ShapeDtypeStruct((n, w.shape[1]), BF16),
        compiler_params=_params(("parallel",)),
        name="sb_in_proj",
    )(x, g.reshape(1, d), w)


def _sb_t_kernel(q_ref, k_ref, v_ref, tri_ref, o_ref, qs_ref, vt_ref, *, tq, tk):
    i = pl.program_id(2)
    n_lb = q_ref.shape[1] // LANES
    cw = tk
    n_cq = tq // cw
    lo = _low_head_lanes()
    strict = lax.broadcasted_iota(jnp.int32, (cw, cw), 0) < lax.broadcasted_iota(jnp.int32, (cw, cw), 1)
    for lb in range(n_lb):
        x = q_ref[:, lb * LANES:(lb + 1) * LANES]
        zero = jnp.zeros_like(x)
        qs_ref[lb, 0] = jnp.where(lo, x, zero)
        qs_ref[lb, 1] = jnp.where(lo, zero, x)

    @pl.when(i == 0)
    def _():
        for lb in range(n_lb):
            vt = v_ref[:, lb * LANES:(lb + 1) * LANES].astype(F32).T
            for e in range(2):
                vt_ref[lb, e] = vt[e * HEAD_DIM:(e + 1) * HEAD_DIM].astype(BF16)

    chains = [(lb, e, c) for lb in range(n_lb) for e in range(2) for c in range(n_cq)]

    def step(base, carry, straddling):
        rows = pl.ds(pl.multiple_of(base, tk), tk)
        active = [n for n, (lb, e, c) in enumerate(chains) if straddling is None or c >= straddling]
        on_diagonal = lambda n: chains[n][2] == straddling
        zs, incls = {}, {}
        new_done, new_acc = list(carry[0]), list(carry[1])
        for n in active:
            lb, e, c = chains[n]
            zs[n] = lax.dot_general(k_ref[rows, lb * LANES:(lb + 1) * LANES],
                                    qs_ref[lb, e, c * cw:(c + 1) * cw, :], _NT, preferred_element_type=F32)
        for n in active:
            zb = zs[n].astype(BF16)
            sp = jnp.maximum(zb, 0) + jnp.log(1 + jnp.exp(-jnp.abs(zb)))
            if on_diagonal(n):
                sp = jnp.where(strict, sp, jnp.zeros_like(sp))
            incls[n] = jnp.dot(tri_ref[...], sp, preferred_element_type=F32)
        for n in active:
            lb, e, c = chains[n]
            a = jnp.exp(zs[n] - incls[n] - carry[0][n])
            if on_diagonal(n):
                a = jnp.where(strict, a, 0.0)
            new_acc[n] = carry[1][n] + jnp.dot(vt_ref[lb, e, :, rows], a.astype(BF16), preferred_element_type=F32)
            new_done[n] = carry[0][n] + incls[n][0:1, :]
        return tuple(new_done), tuple(new_acc)

    carry = (tuple(jnp.zeros((1, cw), F32) for _ in chains), tuple(jnp.zeros((HEAD_DIM, cw), F32) for _ in chains))
    for j in reversed(range(n_cq)):
        carry = step(i * tq + j * tk, carry, j)
    n_below = i * (tq // tk)
    _, acc = lax.fori_loop(0, n_below, lambda t, c: step((n_below - 1 - t) * tk, c, None), carry)
    for lb in range(n_lb):
        for c in range(n_cq):
            out_t = jnp.concatenate([acc[(lb * 2 + e) * n_cq + c] for e in range(2)], axis=0)
            o_ref[c * cw:(c + 1) * cw, lb * LANES:(lb + 1) * LANES] = out_t.T.astype(BF16)


def _sb_attn(qkv, batch, seq, *, tq=1024, tk=256, n_lb=2):
    width = n_lb * LANES
    groups = C_HEADS * HEAD_DIM // width
    q3 = qkv.reshape(batch, seq, qkv.shape[1])
    tri = jnp.asarray(np.triu(np.ones((tk, tk), np.float32)), BF16)
    out = pl.pallas_call(
        functools.partial(_sb_t_kernel, tq=tq, tk=tk),
        grid=(batch, groups, seq // tq),
        in_specs=[pl.BlockSpec((None, tq, width), lambda b, g, i: (b, i, g)),
                  pl.BlockSpec((None, seq, width), lambda b, g, i: (b, 0, groups + g)),
                  pl.BlockSpec((None, seq, width), lambda b, g, i: (b, 0, 2 * groups + g)),
                  pl.BlockSpec((tk, tk), lambda b, g, i: (0, 0))],
        out_specs=pl.BlockSpec((None, tq, width), lambda b, g, i: (b, i, g)),
        out_shape=jax.ShapeDtypeStruct((batch, seq, groups * width), BF16),
        scratch_shapes=[pltpu.VMEM((n_lb, 2, tq, LANES), BF16), pltpu.VMEM((n_lb, 2, HEAD_DIM, seq), BF16)],
        compiler_params=_params(("parallel", "parallel", "arbitrary")),
        name="sb_attn",
    )(q3, q3, q3, tri)
    return out.reshape(batch * seq, groups * width)


def kernel(x, ffn_norm_g, mix_norm_g, ffn_w_gate, ffn_w_up, ffn_w_down, ab_w_in, mla_q_norm_g,
           mla_w_uq, mla_kv_norm_g, mla_w_ukv, ab_w_out, sb_w_in, sb_w_out, final_norm_g):
    batch, seq, d = x.shape
    depth = ffn_norm_g.shape[0]
    h = x.reshape(batch * seq, d)
    bf = lambda w: w.astype(BF16)

    wg, wu, wd = bf(ffn_w_gate), bf(ffn_w_up), bf(ffn_w_down)

    def ffn(h, pre, i, s, final_g=None):
        return _ffn(h, pre, ffn_norm_g[i, s], wg, wu, wd, (i, s), final_g)

    for i in range(depth):
        h = ffn(h, [], i, 0)
        if i % 2 == 0:
            e = i // 2
            a_qkv, qn, qr, kn, kr, v = _ab_in_proj(h, seq, mix_norm_g[i], ab_w_in[e], mla_q_norm_g[e],
                                                   mla_w_uq[e], mla_kv_norm_g[e], mla_w_ukv[e])
            o_a = _dilated_attn(a_qkv, batch, seq)
            o_b = _mla_attn(qn, qr, kn, kr, v, batch, seq)
            w_out = bf(ab_w_out[e])
            na = o_a.shape[1]
            pre = [(o_a, w_out[:na]), (o_b, w_out[na:])]
        else:
            o = i // 2
            qkv = _sb_in_proj(h, mix_norm_g[i], sb_w_in[o])
            pre = [(_sb_attn(qkv, batch, seq), bf(sb_w_out[o]))]
        h = ffn(h, pre, i, 1, final_norm_g if i == depth - 1 else None)
    return h.reshape(batch, seq, d)
```

```python
import functools

import numpy as np
import jax
import jax.numpy as jnp
from jax import lax
from jax.experimental import pallas as pl
from jax.experimental.pallas import tpu as pltpu

F32 = jnp.float32
BF16 = jnp.bfloat16

NORM_EPS = 1e-6
ROPE_THETA = 10000.0
LANES = 128
HEAD_DIM = 64
BLOCK = 128
A_HEADS = 8
A_BRANCHES = ((128, 1), (512, 4), (2048, 16))
B_HEADS = 8
B_NOPE = 64
B_ROPE = 32
B_Q_RANK = 256
B_KV_RANK = 128
C_HEADS = 16
NEG_BIG = -1e30
LOG2E = 1.4426950408889634
VMEM_LIMIT = 58 * 1024 * 1024

_NT = (((1,), (1,)), ((), ()))


def _rms(x, g):
    return x * lax.rsqrt(jnp.mean(x * x, axis=-1, keepdims=True) + NORM_EPS) * g


def _const_spec(shape):
    nd = len(shape)
    return pl.BlockSpec(shape, lambda *_: (0,) * nd, pipeline_mode=pl.Buffered(1))


def _params(sem):
    return pltpu.CompilerParams(dimension_semantics=sem, vmem_limit_bytes=VMEM_LIMIT)


def _ffn_kernel(*refs, n_pre, final_norm, tf):
    x_ref = refs[0]
    pre = refs[1:1 + 2 * n_pre]
    g_ref, wg_ref, wu_ref, wd_ref = refs[1 + 2 * n_pre:5 + 2 * n_pre]
    rest = refs[5 + 2 * n_pre:]
    fg_ref = rest[0] if final_norm else None
    o_ref = rest[-1]

    x = x_ref[...]
    for p in range(n_pre):
        x = x + jnp.dot(pre[2 * p][...], pre[2 * p + 1][...], preferred_element_type=F32)
    h = _rms(x, g_ref[...]).astype(BF16)
    d_ff = wg_ref.shape[1]
    acc = None
    for c in range(d_ff // tf):
        sl = slice(c * tf, (c + 1) * tf)
        gate = jnp.dot(h, wg_ref[:, sl], preferred_element_type=F32)
        up = jnp.dot(h, wu_ref[:, sl], preferred_element_type=F32)
        a = (gate * jax.nn.sigmoid(gate) * up).astype(BF16)
        d = jnp.dot(a, wd_ref[sl, :], preferred_element_type=F32)
        acc = d if acc is None else acc + d
    y = x + 0.5 * acc
    if final_norm:
        y = _rms(y, fg_ref[...])
    o_ref[...] = y


def _ffn(x, pre, g, wg, wu, wd, which, final_g=None, *, tm=512, tf=256):
    n, d = x.shape
    row = lambda i: (i, 0)
    picked = lambda w: pl.BlockSpec((None, None) + w.shape[2:], lambda i: which + (0, 0),
                                    pipeline_mode=pl.Buffered(1))
    args = [x]
    specs = [pl.BlockSpec((tm, d), row)]
    for o, w in pre:
        args += [o, w]
        specs += [pl.BlockSpec((tm, o.shape[1]), row), _const_spec(w.shape)]
    args += [g.reshape(1, d), wg, wu, wd]
    specs += [_const_spec((1, d)), picked(wg), picked(wu), picked(wd)]
    if final_g is not None:
        args.append(final_g.reshape(1, d))
        specs.append(_const_spec((1, d)))
    return pl.pallas_call(
        functools.partial(_ffn_kernel, n_pre=len(pre), final_norm=final_g is not None, tf=tf),
        grid=(n // tm,),
        in_specs=specs,
        out_specs=pl.BlockSpec((tm, d), row),
        out_shape=jax.ShapeDtypeStruct((n, d), F32),
        compiler_params=_params(("parallel",)),
        name="ffn",
    )(*args)


def _ab_in_kernel(x_ref, g_ref, win_ref, qg_ref, wuq_ref, kvg_ref, wukv_ref,
                  cq_ref, sq_ref, ck_ref, sk_ref,
                  a_ref, qn_ref, qr_ref, kn_ref, kr_ref, v_ref, *, scale):
    h = _rms(x_ref[...], g_ref[...]).astype(BF16)
    p = jnp.dot(h, win_ref[...], preferred_element_type=F32)
    na = a_ref.shape[1]
    a_ref[...] = p[:, :na]
    c_q = p[:, na:na + B_Q_RANK]
    c_kv = p[:, na + B_Q_RANK:na + B_Q_RANK + B_KV_RANK]
    k0 = na + B_Q_RANK + B_KV_RANK
    kr_ref[...] = (p[:, k0:k0 + LANES] * ck_ref[...] + p[:, k0 + LANES:k0 + 2 * LANES] * sk_ref[...]).astype(BF16)
    q = jnp.dot(_rms(c_q, qg_ref[...]).astype(BF16), wuq_ref[...], preferred_element_type=F32)
    nn = qn_ref.shape[1]
    nr = qr_ref.shape[1]
    qn_ref[...] = (q[:, :nn] * scale).astype(BF16)
    qr_ref[...] = ((q[:, nn:nn + nr] * cq_ref[...] + q[:, nn + nr:nn + 2 * nr] * sq_ref[...]) * scale).astype(BF16)
    kv = jnp.dot(_rms(c_kv, kvg_ref[...]).astype(BF16), wukv_ref[...], preferred_element_type=F32)
    kn_ref[...] = kv[:, :nn].astype(BF16)
    v_ref[...] = kv[:, nn:].astype(BF16)


def _rot_half_cols(w, width):
    k, n = w.shape
    w3 = w.reshape(k, n // width, 2, width // 2)
    return jnp.stack([-w3[:, :, 1], w3[:, :, 0]], axis=2).reshape(k, n)


def _ab_in_proj(x, seq, g, w_in, q_g, w_uq, kv_g, w_ukv, *, tm=512):
    n, d = x.shape
    a_cols = 3 * A_HEADS * HEAD_DIM
    a_scale = HEAD_DIM ** -0.5
    w_a = jnp.concatenate([w_in[:, :A_HEADS * HEAD_DIM] * a_scale, w_in[:, A_HEADS * HEAD_DIM:a_cols]], axis=1)
    w_lat = w_in[:, a_cols:a_cols + B_Q_RANK + B_KV_RANK]
    w_kr = w_in[:, a_cols + B_Q_RANK + B_KV_RANK:]
    reps = LANES // B_ROPE
    w_full = jnp.concatenate(
        [w_a, w_lat, jnp.tile(w_kr, (1, reps)), jnp.tile(_rot_half_cols(w_kr, B_ROPE), (1, reps))], axis=1).astype(BF16)
    uq = w_uq.reshape(B_Q_RANK, B_HEADS, B_NOPE + B_ROPE)
    uq_n = uq[:, :, :B_NOPE].reshape(B_Q_RANK, B_HEADS * B_NOPE)
    uq_r = uq[:, :, B_NOPE:].reshape(B_Q_RANK, B_HEADS * B_ROPE)
    w_uq_full = jnp.concatenate([uq_n, uq_r, _rot_half_cols(uq_r, B_ROPE)], axis=1).astype(BF16)
    ukv = w_ukv.reshape(B_KV_RANK, B_HEADS, 2, B_NOPE)
    w_ukv_full = jnp.concatenate([ukv[:, :, 0].reshape(B_KV_RANK, -1), ukv[:, :, 1].reshape(B_KV_RANK, -1)], axis=1).astype(BF16)
    inv = ROPE_THETA ** (-jnp.arange(0, B_ROPE, 2, dtype=F32) / B_ROPE)
    ang = jnp.arange(seq, dtype=F32)[:, None] * inv[None, :]
    cos2 = jnp.concatenate([jnp.cos(ang)] * 2, axis=1)
    sin2 = jnp.concatenate([jnp.sin(ang)] * 2, axis=1)
    cq, sq = jnp.tile(cos2, (1, B_HEADS)), jnp.tile(sin2, (1, B_HEADS))
    ck, sk = jnp.tile(cos2, (1, reps)), jnp.tile(sin2, (1, reps))

    row = lambda i: (i, 0)
    per_seq = seq // tm
    pos = lambda i: (i % per_seq, 0)
    nq_n = B_HEADS * B_NOPE
    nq_r = B_HEADS * B_ROPE
    outs = [(a_cols, F32), (nq_n, BF16), (nq_r, BF16), (nq_n, BF16), (LANES, BF16), (nq_n, BF16)]
    return pl.pallas_call(
        functools.partial(_ab_in_kernel, scale=(B_NOPE + B_ROPE) ** -0.5),
        grid=(n // tm,),
        in_specs=[pl.BlockSpec((tm, d), row), _const_spec((1, d)), _const_spec(w_full.shape),
                  _const_spec((1, B_Q_RANK)), _const_spec(w_uq_full.shape),
                  _const_spec((1, B_KV_RANK)), _const_spec(w_ukv_full.shape),
                  pl.BlockSpec((tm, nq_r), pos), pl.BlockSpec((tm, nq_r), pos),
                  pl.BlockSpec((tm, LANES), pos), pl.BlockSpec((tm, LANES), pos)],
        out_specs=[pl.BlockSpec((tm, c), row) for c, _ in outs],
        out_shape=[jax.ShapeDtypeStruct((n, c), dt) for c, dt in outs],
        compiler_params=_params(("parallel",)),
        name="ab_in_proj",
    )(x, g.reshape(1, d), w_full, q_g.reshape(1, -1), w_uq_full, kv_g.reshape(1, -1), w_ukv_full, cq, sq, ck, sk)


def _low_head_lanes():
    return lax.broadcasted_iota(jnp.int32, (1, LANES), 1) < HEAD_DIM


def _stack_heads(x, lo):
    zero = jnp.zeros_like(x)
    return jnp.concatenate([jnp.where(lo, x, zero), jnp.where(lo, zero, x)], axis=0)


def _pv(p, v, lo):
    r = p.shape[0] // 2
    zero = jnp.zeros_like(v)
    return (jnp.dot(p[:r], jnp.where(lo, v, zero), preferred_element_type=F32)
            + jnp.dot(p[r:], jnp.where(lo, zero, v), preferred_element_type=F32))


def _unstack(x, lo):
    r = x.shape[0] // 2
    return jnp.where(lo, x[:r], x[r:])


def _dilated_kernel(*refs, n_lb, unroll):
    q_refs, k_refs, v_refs = refs[:n_lb], refs[n_lb:2 * n_lb], refs[2 * n_lb:3 * n_lb]
    bias_ref, o_ref, qd, kd, vd, lwd, od, lws, os_ = refs[3 * n_lb:]
    seq = o_ref.shape[0]
    lo = _low_head_lanes()

    def run_branch(bi, nb, src_q, src_k, src_v, dst_lw, dst_o):
        def blocks(it, _):
            work = []
            for u in range(unroll):
                idx = it * unroll + u
                base = pl.multiple_of(idx * BLOCK, BLOCK)
                cur = pl.ds(base, BLOCK)
                prv = pl.ds(pl.multiple_of(jnp.maximum(base - BLOCK, 0), BLOCK), BLOCK)
                pen = jnp.where((idx % nb) == 0, NEG_BIG, 0.0).astype(F32)
                work += [(lb, cur, prv, pen) for lb in range(n_lb)]
            scores = []
            for lb, cur, prv, pen in work:
                qs = _stack_heads(src_q[lb][cur, :].astype(BF16), lo)
                s_c = (lax.dot_general(qs, src_k[lb][cur, :].astype(BF16), _NT, preferred_element_type=F32)
                       + bias_ref[bi, lb, :, BLOCK:])
                s_p = None
                if nb > 1:
                    s_p = (lax.dot_general(qs, src_k[lb][prv, :].astype(BF16), _NT, preferred_element_type=F32)
                           + bias_ref[bi, lb, :, :BLOCK] + pen)
                scores.append((s_c, s_p))
            probs = []
            for s_c, s_p in scores:
                m = jnp.max(s_c if s_p is None else jnp.maximum(s_c, s_p), axis=-1, keepdims=True)
                p_c = jnp.exp(s_c - m)
                p_p = None if s_p is None else jnp.exp(s_p - m)
                l = jnp.sum(p_c if s_p is None else p_c + p_p, axis=-1, keepdims=True)
                probs.append((m, l, p_c, p_p))
            for (lb, cur, prv, _), (m, l, p_c, p_p) in zip(work, probs):
                acc = _pv(p_c.astype(BF16), src_v[lb][cur, :].astype(BF16), lo)
                if nb > 1:
                    acc = acc + _pv(p_p.astype(BF16), src_v[lb][prv, :].astype(BF16), lo)
                l2 = _unstack(l, lo)
                dst_o[lb][cur, :] = acc * (1.0 / l2)
                dst_lw[lb][cur, :] = _unstack(m, lo) + jnp.log(l2)
            return 0

        lax.fori_loop(0, seq // BLOCK // unroll, blocks, 0)

    per_lb = lambda ref, *lead: [ref.at[(*lead, lb)] for lb in range(n_lb)]
    for bi, (_, dil) in enumerate(A_BRANCHES):
        sub = seq // dil
        nb = -(-sub // BLOCK)
        if dil == 1:
            run_branch(bi, nb, q_refs, k_refs, v_refs, per_lb(lws, bi), per_lb(os_, bi))
            continue
        for lb in range(n_lb):
            for r in range(dil):
                rows = pl.ds(r, sub, stride=dil)
                qd[lb, r * sub:(r + 1) * sub, :] = q_refs[lb][rows, :].astype(BF16)
                kd[lb, r * sub:(r + 1) * sub, :] = k_refs[lb][rows, :].astype(BF16)
                vd[lb, r * sub:(r + 1) * sub, :] = v_refs[lb][rows, :].astype(BF16)
        run_branch(bi, nb, per_lb(qd), per_lb(kd), per_lb(vd), per_lb(lwd), per_lb(od))
        for lb in range(n_lb):
            for r in range(dil):
                rows = pl.ds(r, sub, stride=dil)
                lws[bi, lb, rows, :] = lwd[lb, r * sub:(r + 1) * sub, :]
                os_[bi, lb, rows, :] = od[lb, r * sub:(r + 1) * sub, :]

    nbr = len(A_BRANCHES)

    def merge(idx, _):
        rows = pl.ds(pl.multiple_of(idx * BLOCK, BLOCK), BLOCK)
        for lb in range(n_lb):
            lw = [lws[i, lb, rows, :] for i in range(nbr)]
            top = functools.reduce(jnp.maximum, lw)
            w = [jnp.exp(x - top) for x in lw]
            num = sum(w[i] * os_[i, lb, rows, :] for i in range(nbr))
            o_ref[rows, lb * LANES:(lb + 1) * LANES] = (num / sum(w)).astype(BF16)
        return 0

    lax.fori_loop(0, seq // BLOCK, merge, 0)


def _alibi_bias():
    slopes = 2.0 ** (-8.0 * np.arange(1, A_HEADS + 1, dtype=np.float64) / A_HEADS)
    qi = np.arange(BLOCK)[:, None]
    kk = np.arange(2 * BLOCK)[None, :]
    rel = qi + BLOCK - kk
    out = np.empty((len(A_BRANCHES), A_HEADS, BLOCK, 2 * BLOCK), np.float32)
    for bi, (window, dil) in enumerate(A_BRANCHES):
        valid = (rel >= 0) & (rel <= window // dil)
        bias = -slopes[:, None, None] * (rel * dil)[None]
        out[bi] = np.where(valid[None], bias, NEG_BIG)
    return jnp.asarray(out.reshape(len(A_BRANCHES), A_HEADS // 2, 2 * BLOCK, 2 * BLOCK))


def _dilated_attn(a_qkv, batch, seq, *, n_lb=2, unroll=4):
    a3 = a_qkv.reshape(batch, seq, a_qkv.shape[1])
    width = n_lb * LANES
    groups = A_HEADS * HEAD_DIM // width
    bias = _alibi_bias()
    blk = lambda t, lb: pl.BlockSpec((None, seq, LANES), lambda g, b: (b, 0, (t * groups + g) * n_lb + lb))
    nbr = len(A_BRANCHES)
    out = pl.pallas_call(
        functools.partial(_dilated_kernel, n_lb=n_lb, unroll=unroll),
        grid=(groups, batch),
        in_specs=[blk(t, lb) for t in range(3) for lb in range(n_lb)]
                 + [pl.BlockSpec((nbr, n_lb, 2 * BLOCK, 2 * BLOCK), lambda g, b: (0, g, 0, 0))],
        out_specs=pl.BlockSpec((None, seq, width), lambda g, b: (b, 0, g)),
        out_shape=jax.ShapeDtypeStruct((batch, seq, groups * width), BF16),
        scratch_shapes=[pltpu.VMEM((n_lb, seq, LANES), BF16)] * 3 + [pltpu.VMEM((n_lb, seq, LANES), F32)] * 2
                       + [pltpu.VMEM((nbr, n_lb, seq, LANES), F32)] * 2,
        compiler_params=_params(("parallel", "parallel")),
        name="dilated_attn",
    )(*([a3] * (3 * n_lb)), bias)
    return out.reshape(batch * seq, groups * width)


V_ROWS = 80


def _mla_t_kernel(qn_ref, qr_ref, kn_ref, kr_ref, v_ref, o_ref, qs_ref, vt_ref, *, tq, tk, cw, skew):
    i = pl.program_id(2)
    n_lb = qn_ref.shape[1] // LANES
    n_cq = tq // cw
    lane = lax.broadcasted_iota(jnp.int32, (1, LANES), 1)
    lo = lane < HEAD_DIM
    causal = lax.broadcasted_iota(jnp.int32, (cw, cw), 0) <= lax.broadcasted_iota(jnp.int32, (cw, cw), 1)

    qr = qr_ref[...]
    for lb in range(n_lb):
        qn = qn_ref[:, lb * LANES:(lb + 1) * LANES]
        for e in range(2):
            slot = 2 * lb + e
            in_slot = (lane >= slot * B_ROPE) & (lane < (slot + 1) * B_ROPE)
            qs_ref[lb, e, :, :LANES] = jnp.where(lo if e == 0 else ~lo, qn, jnp.zeros_like(qn))
            qs_ref[lb, e, :, LANES:] = jnp.where(in_slot, qr, jnp.zeros_like(qr))

    @pl.when(i == 0)
    def _():
        for lb in range(n_lb):
            vt = v_ref[:, lb * LANES:(lb + 1) * LANES].astype(F32).T
            for e in range(2):
                vt_ref[lb, e, :HEAD_DIM, :] = vt[e * HEAD_DIM:(e + 1) * HEAD_DIM].astype(BF16)
                vt_ref[lb, e, HEAD_DIM:, :] = jnp.ones((V_ROWS - HEAD_DIM, vt.shape[1]), BF16)

    chains = [(lb, e, c) for lb in range(n_lb) for e in range(2) for c in range(n_cq)]

    def step(kb, carry, diagonal):
        base = pl.multiple_of(kb * (tq if diagonal else tk), tk)
        width = lambda c: (c + 1) * cw if diagonal else tk
        n_ch = len(chains)
        ss, new_m, alphas, ps, new_acc = ([None] * n_ch for _ in range(5))

        def scores(n):
            lb, e, c = chains[n]
            rows = pl.ds(base, width(c))
            k = jnp.concatenate([kn_ref[rows, lb * LANES:(lb + 1) * LANES], kr_ref[rows, :]], axis=1)
            ss[n] = lax.dot_general(k, qs_ref[lb, e, c * cw:(c + 1) * cw, :], _NT, preferred_element_type=F32)

        def probabilities(n):
            lb, e, c = chains[n]
            s = ss[n]
            if diagonal:
                last = jnp.where(causal, s[-cw:], NEG_BIG)
                s = last if c == 0 else jnp.concatenate([s[:-cw], last], axis=0)
            new_m[n] = jnp.maximum(carry[0][n], jnp.max(s, axis=0, keepdims=True))
            alphas[n] = jnp.exp(carry[0][n] - new_m[n])
            ps[n] = jnp.exp(s - new_m[n]).astype(BF16)

        def outputs(n):
            lb, e, c = chains[n]
            vt = vt_ref[lb, e, :, pl.ds(base, width(c))]
            new_acc[n] = alphas[n] * carry[1][n] + jnp.dot(vt, ps[n], preferred_element_type=F32)

        for n in range(n_ch + 2 * skew):
            for stage, m in ((scores, n), (probabilities, n - skew), (outputs, n - 2 * skew)):
                if 0 <= m < n_ch:
                    stage(m)
        return tuple(new_m), tuple(new_acc)

    init = (tuple(jnp.full((1, cw), NEG_BIG, F32) for _ in chains),
            tuple(jnp.zeros((V_ROWS, cw), F32) for _ in chains))
    carry = lax.fori_loop(0, i * (tq // tk), functools.partial(step, diagonal=False), init)
    _, acc = step(i, carry, True)
    for lb in range(n_lb):
        for c in range(n_cq):
            heads = [acc[(lb * 2 + e) * n_cq + c] for e in range(2)]
            out_t = jnp.concatenate([a[:HEAD_DIM] / a[HEAD_DIM:HEAD_DIM + 1] for a in heads], axis=0)
            o_ref[c * cw:(c + 1) * cw, lb * LANES:(lb + 1) * LANES] = out_t.T.astype(BF16)


def _mla_attn(qn, qr, kn, kr, v, batch, seq, *, tq=1024, tk=512, skew=8):
    width = (LANES // B_ROPE) * B_NOPE
    groups = qn.shape[1] // width
    r3 = lambda t: t.reshape(batch, seq, t.shape[1])
    out = pl.pallas_call(
        functools.partial(_mla_t_kernel, tq=tq, tk=tk, cw=2 * LANES, skew=skew),
        grid=(batch, groups, seq // tq),
        in_specs=[pl.BlockSpec((None, tq, width), lambda b, g, i: (b, i, g)),
                  pl.BlockSpec((None, tq, LANES), lambda b, g, i: (b, i, g)),
                  pl.BlockSpec((None, seq, width), lambda b, g, i: (b, 0, g)),
                  pl.BlockSpec((None, seq, LANES), lambda b, g, i: (b, 0, 0)),
                  pl.BlockSpec((None, seq, width), lambda b, g, i: (b, 0, g))],
        out_specs=pl.BlockSpec((None, tq, width), lambda b, g, i: (b, i, g)),
        out_shape=jax.ShapeDtypeStruct((batch, seq, groups * width), BF16),
        scratch_shapes=[pltpu.VMEM((width // LANES, 2, tq, 2 * LANES), BF16),
                        pltpu.VMEM((width // LANES, 2, V_ROWS, seq), BF16)],
        compiler_params=_params(("parallel", "parallel", "arbitrary")),
        name="mla_attn",
    )(r3(qn), r3(qr), r3(kn), r3(kr), r3(v))
    return out.reshape(batch * seq, groups * width)


def _norm_proj_kernel(x_ref, g_ref, w_ref, o_ref):
    h = _rms(x_ref[...], g_ref[...]).astype(BF16)
    o_ref[...] = jnp.dot(h, w_ref[...], preferred_element_type=F32).astype(o_ref.dtype)


def _sb_in_proj(x, g, w_in, *, tm=512):
    n, d = x.shape
    nq = C_HEADS * HEAD_DIM
    w = jnp.concatenate([w_in[:, :nq] * HEAD_DIM ** -0.5, w_in[:, nq:]], axis=1).astype(BF16)
    return pl.pallas_call(
        _norm_proj_kernel,
        grid=(n // tm,),
        in_specs=[pl.BlockSpec((tm, d), lambda i: (i, 0)), _const_spec((1, d)), _const_spec(w.shape)],
        out_specs=pl.BlockSpec((tm, w.shape[1]), lambda i: (i, 0)),
        out_shape=jax.ShapeDtypeStruct((n, w.shape[1]), BF16),
        compiler_params=_params(("parallel",)),
        name="sb_in_proj",
    )(x, g.reshape(1, d), w)


def _sb_t_kernel(q_ref, k_ref, v_ref, tri_ref, o_ref, qs_ref, vt_ref, *, tq, tk, skew):
    i = pl.program_id(2)
    n_lb = q_ref.shape[1] // LANES
    cw = tk
    n_cq = tq // cw
    lo = _low_head_lanes()
    strict = lax.broadcasted_iota(jnp.int32, (cw, cw), 0) < lax.broadcasted_iota(jnp.int32, (cw, cw), 1)
    for lb in range(n_lb):
        x = q_ref[:, lb * LANES:(lb + 1) * LANES]
        zero = jnp.zeros_like(x)
        qs_ref[lb, 0] = jnp.where(lo, x, zero)
        qs_ref[lb, 1] = jnp.where(lo, zero, x)

    @pl.when(i == 0)
    def _():
        for lb in range(n_lb):
            vt = v_ref[:, lb * LANES:(lb + 1) * LANES].astype(F32).T
            for e in range(2):
                vt_ref[lb, e] = vt[e * HEAD_DIM:(e + 1) * HEAD_DIM].astype(BF16)

    chains = [(lb, e, c) for lb in range(n_lb) for e in range(2) for c in range(n_cq)]

    def step(base, carry, straddling):
        rows = pl.ds(pl.multiple_of(base, tk), tk)
        active = [n for n, (lb, e, c) in enumerate(chains) if straddling is None or c >= straddling]
        on_diagonal = lambda n: chains[n][2] == straddling
        zs, incls = {}, {}
        new_done, new_acc = list(carry[0]), list(carry[1])
        def scores(n):
            lb, e, c = chains[n]
            zs[n] = lax.dot_general(k_ref[rows, lb * LANES:(lb + 1) * LANES],
                                    qs_ref[lb, e, c * cw:(c + 1) * cw, :], _NT, preferred_element_type=F32)

        def cumulative(n):
            zb = zs[n].astype(BF16)
            sp = jnp.maximum(zb, 0) + jnp.log(1 + jnp.exp(-jnp.abs(zb)))
            if on_diagonal(n):
                sp = jnp.where(strict, sp, jnp.zeros_like(sp))
            incls[n] = jnp.dot(tri_ref[...], sp, preferred_element_type=F32)

        def outputs(n):
            lb, e, c = chains[n]
            a = jnp.exp(zs[n] - incls[n] - carry[0][n])
            if on_diagonal(n):
                a = jnp.where(strict, a, 0.0)
            new_acc[n] = carry[1][n] + jnp.dot(vt_ref[lb, e, :, rows], a.astype(BF16), preferred_element_type=F32)
            new_done[n] = carry[0][n] + incls[n][0:1, :]

        for k in range(len(active) + 2 * skew):
            for stage, m in ((scores, k), (cumulative, k - skew), (outputs, k - 2 * skew)):
                if 0 <= m < len(active):
                    stage(active[m])
        return tuple(new_done), tuple(new_acc)

    carry = (tuple(jnp.zeros((1, cw), F32) for _ in chains), tuple(jnp.zeros((HEAD_DIM, cw), F32) for _ in chains))
    for j in reversed(range(n_cq)):
        carry = step(i * tq + j * tk, carry, j)
    n_below = i * (tq // tk)
    _, acc = lax.fori_loop(0, n_below, lambda t, c: step((n_below - 1 - t) * tk, c, None), carry)
    for lb in range(n_lb):
        for c in range(n_cq):
            out_t = jnp.concatenate([acc[(lb * 2 + e) * n_cq + c] for e in range(2)], axis=0)
            o_ref[c * cw:(c + 1) * cw, lb * LANES:(lb + 1) * LANES] = out_t.T.astype(BF16)


def _sb_attn(qkv, batch, seq, *, tq=1024, tk=256, n_lb=2, skew=8):
    width = n_lb * LANES
    groups = C_HEADS * HEAD_DIM // width
    q3 = qkv.reshape(batch, seq, qkv.shape[1])
    tri = jnp.asarray(np.triu(np.ones((tk, tk), np.float32)), BF16)
    out = pl.pallas_call(
        functools.partial(_sb_t_kernel, tq=tq, tk=tk, skew=skew),
        grid=(batch, groups, seq // tq),
        in_specs=[pl.BlockSpec((None, tq, width), lambda b, g, i: (b, i, g)),
                  pl.BlockSpec((None, seq, width), lambda b, g, i: (b, 0, groups + g)),
                  pl.BlockSpec((None, seq, width), lambda b, g, i: (b, 0, 2 * groups + g)),
                  pl.BlockSpec((tk, tk), lambda b, g, i: (0, 0))],
        out_specs=pl.BlockSpec((None, tq, width), lambda b, g, i: (b, i, g)),
        out_shape=jax.ShapeDtypeStruct((batch, seq, groups * width), BF16),
        scratch_shapes=[pltpu.VMEM((n_lb, 2, tq, LANES), BF16), pltpu.VMEM((n_lb, 2, HEAD_DIM, seq), BF16)],
        compiler_params=_params(("parallel", "parallel", "arbitrary")),
        name="sb_attn",
    )(q3, q3, q3, tri)
    return out.reshape(batch * seq, groups * width)


def kernel(x, ffn_norm_g, mix_norm_g, ffn_w_gate, ffn_w_up, ffn_w_down, ab_w_in, mla_q_norm_g,
           mla_w_uq, mla_kv_norm_g, mla_w_ukv, ab_w_out, sb_w_in, sb_w_out, final_norm_g):
    batch, seq, d = x.shape
    depth = ffn_norm_g.shape[0]
    h = x.reshape(batch * seq, d)
    bf = lambda w: w.astype(BF16)

    wg, wu, wd = bf(ffn_w_gate), bf(ffn_w_up), bf(ffn_w_down)

    def ffn(h, pre, i, s, final_g=None):
        return _ffn(h, pre, ffn_norm_g[i, s], wg, wu, wd, (i, s), final_g)

    for i in range(depth):
        h = ffn(h, [], i, 0)
        if i % 2 == 0:
            e = i // 2
            a_qkv, qn, qr, kn, kr, v = _ab_in_proj(h, seq, mix_norm_g[i], ab_w_in[e], mla_q_norm_g[e],
                                                   mla_w_uq[e], mla_kv_norm_g[e], mla_w_ukv[e])
            o_a = _dilated_attn(a_qkv, batch, seq)
            o_b = _mla_attn(qn, qr, kn, kr, v, batch, seq)
            w_out = bf(ab_w_out[e])
            na = o_a.shape[1]
            pre = [(o_a, w_out[:na]), (o_b, w_out[na:])]
        else:
            o = i // 2
            qkv = _sb_in_proj(h, mix_norm_g[i], sb_w_in[o])
            pre = [(_sb_attn(qkv, batch, seq), bf(sb_w_out[o]))]
        h = ffn(h, pre, i, 1, final_norm_g if i == depth - 1 else None)
    return h.reshape(batch, seq, d)
```
